```python
import jax, jax.numpy as jnp
from jax import lax
import numpy as np

D_MODEL = 1024
BATCH = 8
SEQ = 4096
DEPTH = 2

HEAD_DIM = 64
GRID_W = 64
NA_HEADS = D_MODEL // (4 * HEAD_DIM)
NA_WIN_H = 8
NA_WIN_W = 16
MLA_HEADS = 3 * D_MODEL // (8 * HEAD_DIM)
MLA_Q_LORA = D_MODEL // 4
MLA_KV_LORA = D_MODEL // 8
MLA_NOPE = 64
MLA_ROPE = 32
MLA_V = 64
MLA_BLOCK = 128
SWA_HEADS = 3 * D_MODEL // (8 * HEAD_DIM)
SWA_KV_HEADS = 2
SWA_WINDOW = 128
SWA_BLOCK = 128
ROPE_THETA = 10000.0
N_EXPERTS = 16
EC_CAPACITY = 2
D_EXPERT = D_MODEL
RMS_EPS = 1e-6
NEG_INF = -1e30

A_W = NA_HEADS * HEAD_DIM
B_W = MLA_HEADS * MLA_V
C_W = SWA_HEADS * HEAD_DIM
D_MIX = A_W + B_W + C_W
IN_WIDTHS = (A_W, A_W, A_W,
             MLA_Q_LORA, MLA_KV_LORA, MLA_ROPE,
             SWA_HEADS * HEAD_DIM, SWA_KV_HEADS * HEAD_DIM, SWA_KV_HEADS * HEAD_DIM)
IN_COLS = sum(IN_WIDTHS)
IN_OFFSETS = tuple(int(v) for v in np.cumsum(IN_WIDTHS)[:-1])

kernel_name = 'hybrid_na_mla_swa_ec_encoder'


def rms_norm(x, g):
    x32 = x.astype(jnp.float32)
    y = x32 * lax.rsqrt(jnp.mean(x32 * x32, axis=-1, keepdims=True) + RMS_EPS)
    return (y * g.astype(jnp.float32)).astype(x.dtype)


def rope_tables(seq, dim, dtype):
    inv = 1.0 / (ROPE_THETA ** (jnp.arange(0, dim, 2, dtype=jnp.float32) / dim))
    ang = jnp.arange(seq, dtype=jnp.float32)[:, None] * inv[None, :]
    return jnp.cos(ang).astype(dtype), jnp.sin(ang).astype(dtype)


def apply_rope(x, cos, sin):
    x1, x2 = jnp.split(x, 2, axis=-1)
    c = cos[None, :, None, :]
    s = sin[None, :, None, :]
    return jnp.concatenate([x1 * c - x2 * s, x2 * c + x1 * s], axis=-1)


def neighbourhood_attention(q, k, v, rpb):
    B, S, _ = q.shape
    rows = S // GRID_W
    wh = min(NA_WIN_H, rows)
    ww = NA_WIN_W
    shp = (B, rows, GRID_W, NA_HEADS, HEAD_DIM)
    qg, kg, vg = q.reshape(shp), k.reshape(shp), v.reshape(shp)
    cols = jnp.arange(GRID_W)
    col_start = jnp.clip(cols - ww // 2, 0, GRID_W - ww)
    col_idx = col_start[:, None] + jnp.arange(ww)[None, :]
    dc = col_idx - cols[:, None] + (NA_WIN_W - 1)
    scale = HEAD_DIM ** -0.5

    def one_row(args):
        r, q_row = args
        rs = jnp.clip(r - wh // 2, 0, rows - wh)
        k_rows = lax.dynamic_slice_in_dim(kg, rs, wh, axis=1)
        v_rows = lax.dynamic_slice_in_dim(vg, rs, wh, axis=1)
        k_win = k_rows[:, :, col_idx]
        v_win = v_rows[:, :, col_idx]
        dr = rs + jnp.arange(wh) - r + (NA_WIN_H - 1)
        bias = rpb[:, dr][:, :, dc].transpose(0, 2, 1, 3)
        s = jnp.einsum('bchd,brcjhd->bhcrj', q_row, k_win).astype(jnp.float32) * scale
        s = s + bias[None].astype(jnp.float32)
        p = jax.nn.softmax(s.reshape(B, NA_HEADS, GRID_W, wh * ww), axis=-1)
        p = p.reshape(s.shape).astype(v.dtype)
        return jnp.einsum('bhcrj,brcjhd->bchd', p, v_win)

    out = lax.map(one_row, (jnp.arange(rows), qg.swapaxes(0, 1)))
    return out.swapaxes(0, 1).reshape(B, S, NA_HEADS * HEAD_DIM)


def latent_attention(c_q, c_kv, k_rope, q_norm, w_uq, kv_norm, w_ukv, cos_r, sin_r):
    B, S, _ = c_q.shape
    q = (rms_norm(c_q, q_norm) @ w_uq).reshape(B, S, MLA_HEADS, MLA_NOPE + MLA_ROPE)
    q_nope = q[..., :MLA_NOPE]
    q_pe = apply_rope(q[..., MLA_NOPE:], cos_r, sin_r)
    kv = (rms_norm(c_kv, kv_norm) @ w_ukv).reshape(B, S, MLA_HEADS, MLA_NOPE + MLA_V)
    k_nope = kv[..., :MLA_NOPE]
    v = kv[..., MLA_NOPE:]
    k_pe = apply_rope(k_rope[:, :, None, :], cos_r, sin_r)[:, :, 0]
    nb = S // MLA_BLOCK
    scale = (MLA_NOPE + MLA_ROPE) ** -0.5

    def to_blocks(t):
        return t.reshape(B, nb, MLA_BLOCK, *t.shape[2:]).swapaxes(0, 1)

    def one_block(args):
        qn, qp = args
        s = (jnp.einsum('bqhd,bkhd->bhqk', qn, k_nope)
             + jnp.einsum('bqhd,bkd->bhqk', qp, k_pe)).astype(jnp.float32) * scale
        p = jax.nn.softmax(s, axis=-1).astype(v.dtype)
        return jnp.einsum('bhqk,bkhd->bqhd', p, v)

    o = lax.map(one_block, (to_blocks(q_nope), to_blocks(q_pe)))
    return o.swapaxes(0, 1).reshape(B, S, MLA_HEADS * MLA_V)


def window_gqa(q, k, v, sink, cos, sin):
    B, S, _ = q.shape
    G = SWA_HEADS // SWA_KV_HEADS
    q = apply_rope(q.reshape(B, S, SWA_HEADS, HEAD_DIM), cos, sin)
    k = apply_rope(k.reshape(B, S, SWA_KV_HEADS, HEAD_DIM), cos, sin)
    v = v.reshape(B, S, SWA_KV_HEADS, HEAD_DIM)
    nb = S // SWA_BLOCK
    n_side = -(-SWA_WINDOW // SWA_BLOCK)
    pad = n_side * SWA_BLOCK
    n_keys = (2 * n_side + 1) * SWA_BLOCK

    def band(t):
        tp = jnp.pad(t, ((0, 0), (pad, pad), (0, 0), (0, 0)))
        tp = tp.reshape(B, nb + 2 * n_side, SWA_BLOCK, SWA_KV_HEADS, HEAD_DIM)
        return jnp.concatenate([tp[:, i:i + nb] for i in range(2 * n_side + 1)], axis=2)

    kb, vb = band(k), band(v)
    qb = q.reshape(B, nb, SWA_BLOCK, SWA_KV_HEADS, G, HEAD_DIM)
    s = jnp.einsum('bnqkgd,bnjkd->bnkgqj', qb, kb).astype(jnp.float32) * (HEAD_DIM ** -0.5)
    blk = jnp.arange(nb)[:, None, None] * SWA_BLOCK
    qpos = blk + jnp.arange(SWA_BLOCK)[None, :, None]
    kpos = blk - pad + jnp.arange(n_keys)[None, None, :]
    valid = (jnp.abs(qpos - kpos) <= SWA_WINDOW) & (kpos >= 0) & (kpos < S)
    s = jnp.where(valid[None, :, None, None], s, NEG_INF)
    sink_logit = jnp.broadcast_to(sink.astype(jnp.float32).reshape(1, 1, SWA_KV_HEADS, G, 1, 1),
                                  s.shape[:-1] + (1,))
    p = jax.nn.softmax(jnp.concatenate([s, sink_logit], axis=-1), axis=-1)[..., :-1].astype(v.dtype)
    o = jnp.einsum('bnkgqj,bnjkd->bnqkgd', p, vb)
    return o.reshape(B, S, SWA_HEADS * HEAD_DIM)


def expert_choice_ffn(xn, w_router, w_gate, w_up, w_down):
    B, S, D = xn.shape
    cap = EC_CAPACITY * S // N_EXPERTS
    aff = jax.nn.softmax((xn @ w_router).astype(jnp.float32), axis=-1)
    g, idx = lax.top_k(aff.swapaxes(1, 2), cap)
    bidx = jnp.arange(B)[:, None, None]
    xs = xn[bidx, idx]
    hid = jax.nn.silu(jnp.einsum('becd,edf->becf', xs, w_gate)) * jnp.einsum('becd,edf->becf', xs, w_up)
    y = jnp.einsum('becf,efd->becd', hid, w_down) * g[..., None].astype(xn.dtype)
    return jnp.zeros_like(xn).at[bidx, idx].add(y)


def setup_inputs(seed: int = 0) -> dict:
    key = jax.random.key(seed)
    ks = jax.random.split(key, 17)
    f32 = jnp.float32

    def nrm(k, shape, scale):
        return jax.random.normal(k, shape, f32) * scale

    def gain(k, shape):
        return 1.0 + 0.05 * jax.random.normal(k, shape, f32)

    return {
        'x': nrm(ks[0], (BATCH, SEQ, D_MODEL), 1.0),
        'attn_norm': gain(ks[1], (DEPTH, D_MODEL)),
        'w_in': nrm(ks[2], (DEPTH, D_MODEL, IN_COLS), D_MODEL ** -0.5),
        'na_rpb': nrm(ks[3], (DEPTH, NA_HEADS, 2 * NA_WIN_H - 1, 2 * NA_WIN_W - 1), 0.1),
        'mla_q_norm': gain(ks[4], (DEPTH, MLA_Q_LORA)),
        'mla_w_uq': nrm(ks[5], (DEPTH, MLA_Q_LORA, MLA_HEADS * (MLA_NOPE + MLA_ROPE)), MLA_Q_LORA ** -0.5),
        'mla_kv_norm': gain(ks[6], (DEPTH, MLA_KV_LORA)),
        'mla_w_ukv': nrm(ks[7], (DEPTH, MLA_KV_LORA, MLA_HEADS * (MLA_NOPE + MLA_V)), MLA_KV_LORA ** -0.5),
        'swa_sink': nrm(ks[8], (DEPTH, SWA_HEADS), 1.0),
        'group_norm': gain(ks[9], (DEPTH, D_MIX)),
        'w_out': nrm(ks[10], (DEPTH, D_MIX, D_MODEL), D_MIX ** -0.5),
        'ffn_norm': gain(ks[11], (DEPTH, D_MODEL)),
        'w_router': nrm(ks[12], (DEPTH, D_MODEL, N_EXPERTS), D_MODEL ** -0.5),
        'w_gate': nrm(ks[13], (DEPTH, N_EXPERTS, D_MODEL, D_EXPERT), D_MODEL ** -0.5),
        'w_up': nrm(ks[14], (DEPTH, N_EXPERTS, D_MODEL, D_EXPERT), D_MODEL ** -0.5),
        'w_down': nrm(ks[15], (DEPTH, N_EXPERTS, D_EXPERT, D_MODEL), D_EXPERT ** -0.5),
        'final_norm': gain(ks[16], (D_MODEL,)),
    }


def reference(x, attn_norm, w_in, na_rpb, mla_q_norm, mla_w_uq, mla_kv_norm, mla_w_ukv, swa_sink,
              group_norm, w_out, ffn_norm, w_router, w_gate, w_up, w_down, final_norm):
    B, S, _ = x.shape
    cos, sin = rope_tables(S, HEAD_DIM, x.dtype)
    cos_r, sin_r = rope_tables(S, MLA_ROPE, x.dtype)
    h = x
    for l in range(DEPTH):
        xn = rms_norm(h, attn_norm[l])
        proj = xn @ w_in[l]
        a_q, a_k, a_v, b_cq, b_ckv, b_kr, c_q, c_k, c_v = jnp.split(proj, IN_OFFSETS, axis=-1)
        o_a = neighbourhood_attention(a_q, a_k, a_v, na_rpb[l])
        o_b = latent_attention(b_cq, b_ckv, b_kr, mla_q_norm[l], mla_w_uq[l],
                               mla_kv_norm[l], mla_w_ukv[l], cos_r, sin_r)
        o_c = window_gqa(c_q, c_k, c_v, swa_sink[l], cos, sin)
        gn = group_norm[l]
        o = jnp.concatenate([rms_norm(o_a, gn[:A_W]),
                             rms_norm(o_b, gn[A_W:A_W + B_W]),
                             rms_norm(o_c, gn[A_W + B_W:])], axis=-1)
        h = h + o @ w_out[l]
        h = h + expert_choice_ffn(rms_norm(h, ffn_norm[l]), w_router[l], w_gate[l], w_up[l], w_down[l])
    return rms_norm(h, final_norm)
```

```python
import functools

import jax
import jax.numpy as jnp
import numpy as np
from jax import lax
from jax.experimental import pallas as pl
from jax.experimental.pallas import tpu as pltpu

F32 = jnp.float32
BF16 = jnp.bfloat16

D_MODEL = 1024
SEQ = 4096
HEAD_DIM = 64
GRID_W = 64
NA_HEADS = 4
NA_WIN_H = 8
NA_WIN_W = 16
MLA_HEADS = 6
MLA_Q_LORA = 256
MLA_KV_LORA = 128
MLA_NOPE = 64
MLA_ROPE = 32
MLA_V = 64
SWA_HEADS = 6
SWA_KV_HEADS = 2
SWA_GROUP = SWA_HEADS // SWA_KV_HEADS
SWA_WINDOW = 128
ROPE_THETA = 10000.0
N_EXPERTS = 16
EC_CAPACITY = 2
CAP = EC_CAPACITY * SEQ // N_EXPERTS
RMS_EPS = 1e-6
NEG_INF = -1e30

A_W = NA_HEADS * HEAD_DIM
B_W = MLA_HEADS * MLA_V
C_W = SWA_HEADS * HEAD_DIM

LANES = 128
ROW_TILE = 512
VMEM_LIMIT = 56 * 1024 * 1024

P_AQ, P_AK, P_AV = 0, 256, 512
P_CQ, P_CKV = 768, 1024
P_SQ, P_SK, P_SV = 1152, 1536, 1664
P_KR = 1792
P_COLS = 1920

NA_QROWS = 8
NA_KROWS = 16
NA_TQ = NA_QROWS * GRID_W
NA_TK = NA_KROWS * GRID_W
MLA_TQ = 256
SWA_TQ = 512
SWA_TK = SWA_TQ + 2 * SWA_WINDOW


def _cparams(sem):
    return pltpu.CompilerParams(dimension_semantics=sem, vmem_limit_bytes=VMEM_LIMIT)


def _rms(x, gain):
    ms = jnp.mean(x * x, axis=-1, keepdims=True)
    return x * lax.rsqrt(ms + RMS_EPS) * gain


def _nt_dot(a, b):
    return lax.dot_general(a, b, (((1,), (1,)), ((), ())), preferred_element_type=F32)


def _proj_kernel(x_ref, gain_ref, win_ref, qn_ref, wq_ref, kvn_ref, wkv_ref, rs_ref, rm_ref,
                 aq_ref, ak_ref, av_ref, mq_ref, mk_ref, mv_ref, sq_ref, sk_ref, sv_ref):
    xn = _rms(x_ref[...], gain_ref[...]).astype(BF16)
    proj = jnp.dot(xn, win_ref[...], preferred_element_type=F32)
    lane = lax.broadcasted_iota(jnp.int32, (1, LANES), 1)
    lo_half = lane < HEAD_DIM

    def rope(grp, tab_ref, half):
        c = tab_ref[:, 0:LANES]
        sa = tab_ref[:, LANES:2 * LANES]
        sb = tab_ref[:, 2 * LANES:3 * LANES]
        return grp * c + pltpu.roll(grp, half, 1) * sa + pltpu.roll(grp, LANES - half, 1) * sb

    def split_heads(grp):
        zero = jnp.zeros_like(grp)
        return jnp.where(lo_half, grp, zero).astype(BF16), jnp.where(lo_half, zero, grp).astype(BF16)

    na_scale = HEAD_DIM ** -0.5
    for p in range(NA_HEADS // 2):
        grp = proj[:, P_AQ + p * LANES:P_AQ + (p + 1) * LANES] * na_scale
        q0, q1 = split_heads(grp)
        aq_ref[:, (2 * p) * LANES:(2 * p + 1) * LANES] = q0
        aq_ref[:, (2 * p + 1) * LANES:(2 * p + 2) * LANES] = q1
    ak_ref[...] = proj[:, P_AK:P_AK + A_W].astype(BF16)
    av_ref[...] = proj[:, P_AV:P_AV + A_W].astype(BF16)

    cq = _rms(proj[:, P_CQ:P_CQ + MLA_Q_LORA], qn_ref[...]).astype(BF16)
    q = jnp.dot(cq, wq_ref[...], preferred_element_type=F32) * ((MLA_NOPE + MLA_ROPE) ** -0.5)
    ckv = _rms(proj[:, P_CKV:P_CKV + MLA_KV_LORA], kvn_ref[...]).astype(BF16)
    kv = jnp.dot(ckv, wkv_ref[...], preferred_element_type=F32)
    kr = rope(proj[:, P_KR:P_KR + LANES], rm_ref, MLA_ROPE // 2)
    for h in range(MLA_HEADS):
        sl = slice(h * LANES, (h + 1) * LANES)
        mq_ref[:, sl] = rope(q[:, sl], rm_ref, MLA_ROPE // 2).astype(BF16)
        mk_ref[:, sl] = (kv[:, sl] + kr).astype(BF16)
    mv_ref[...] = kv[:, MLA_HEADS * LANES:MLA_HEADS * LANES + B_W].astype(BF16)

    swa_scale = HEAD_DIM ** -0.5
    for g in range(SWA_GROUP):
        grp = rope(proj[:, P_SQ + g * LANES:P_SQ + (g + 1) * LANES], rs_ref, HEAD_DIM // 2) * swa_scale
        q0, q1 = split_heads(grp)
        sq_ref[:, (2 * g) * LANES:(2 * g + 1) * LANES] = q0
        sq_ref[:, (2 * g + 1) * LANES:(2 * g + 2) * LANES] = q1
    sk_ref[...] = rope(proj[:, P_SK:P_SK + LANES], rs_ref, HEAD_DIM // 2).astype(BF16)
    sv_ref[...] = proj[:, P_SV:P_SV + LANES].astype(BF16)


def _proj_call(x2, gain, win, qn, wq, kvn, wkv, rope_s, rope_m):
    n = x2.shape[0]
    tm = ROW_TILE
    seq_tiles = SEQ // tm
    row = lambda w: pl.BlockSpec((tm, w), lambda i: (i, 0))
    full = lambda a: pl.BlockSpec(a.shape, lambda i: (0,) * a.ndim)
    pos = lambda w: pl.BlockSpec((tm, w), lambda i: (i % seq_tiles, 0))
    widths = (2 * A_W, A_W, A_W, MLA_HEADS * LANES, MLA_HEADS * LANES, B_W, 2 * C_W, LANES, LANES)
    return pl.pallas_call(
        _proj_kernel,
        grid=(n // tm,),
        in_specs=[row(D_MODEL), full(gain), full(win), full(qn), full(wq), full(kvn), full(wkv),
                  pos(3 * LANES), pos(3 * LANES)],
        out_specs=[row(w) for w in widths],
        out_shape=[jax.ShapeDtypeStruct((n, w), BF16) for w in widths],
        compiler_params=_cparams(("arbitrary",)),
        name="norm_in_proj",
    )(x2, gain, win, qn, wq, kvn, wkv, rope_s, rope_m)


def _na_key_start(j):
    rows = SEQ // GRID_W
    return jnp.clip(NA_QROWS * j - NA_WIN_H // 2, 0, rows - NA_KROWS)


def _na_kernel(q_ref, k_ref, v_ref, bias_ref, o_ref):
    j = pl.program_id(1)
    ks = pl.multiple_of(_na_key_start(j) * GRID_W, 256)
    kwin = k_ref[0, pl.ds(ks, NA_TK), :]
    vwin = v_ref[0, pl.ds(ks, NA_TK), :]
    lo_half = lax.broadcasted_iota(jnp.int32, (1, LANES), 1) < HEAD_DIM
    for p in range(NA_HEADS // 2):
        kp = kwin[:, p * LANES:(p + 1) * LANES]
        vp = vwin[:, p * LANES:(p + 1) * LANES]
        outs = []
        for hh in range(2):
            h = 2 * p + hh
            s = _nt_dot(q_ref[0, :, h * LANES:(h + 1) * LANES], kp) + bias_ref[0, h]
            m = jnp.max(s, axis=-1, keepdims=True)
            e = jnp.exp(s - m)
            l = jnp.sum(e, axis=-1, keepdims=True)
            outs.append(jnp.dot(e.astype(BF16), vp, preferred_element_type=F32) / l)
        o_ref[0, :, p * LANES:(p + 1) * LANES] = jnp.where(lo_half, outs[0], outs[1])


def _na_call(aq, ak, av, bias):
    b = aq.shape[0]
    nj = SEQ // NA_TQ

    def variant(bi, j):
        return (jnp.where(j == 0, 0, jnp.where(j == nj - 1, 2, 1)), 0, 0, 0)

    return pl.pallas_call(
        _na_kernel,
        grid=(b, nj),
        in_specs=[pl.BlockSpec((1, NA_TQ, 2 * A_W), lambda bi, j: (bi, j, 0)),
                  pl.BlockSpec((1, SEQ, A_W), lambda bi, j: (bi, 0, 0)),
                  pl.BlockSpec((1, SEQ, A_W), lambda bi, j: (bi, 0, 0)),
                  pl.BlockSpec((1, NA_HEADS, NA_TQ, NA_TK), variant)],
        out_specs=pl.BlockSpec((1, NA_TQ, A_W), lambda bi, j: (bi, j, 0)),
        out_shape=jax.ShapeDtypeStruct((b, SEQ, A_W), F32),
        compiler_params=_cparams(("arbitrary", "arbitrary")),
        name="neighbourhood_attn",
    )(aq, ak, av, bias)


def _na_bias_tables(rpb):
    rows = SEQ // GRID_W
    nj = SEQ // NA_TQ
    dr_l, rv_l = [], []
    for j in (0, 1, nj - 1):
        ks = int(np.clip(NA_QROWS * j - NA_WIN_H // 2, 0, rows - NA_KROWS))
        r = NA_QROWS * j + np.arange(NA_QROWS)[:, None]
        kr = ks + np.arange(NA_KROWS)[None, :]
        rs = np.clip(r - NA_WIN_H // 2, 0, rows - NA_WIN_H)
        rv_l.append((kr >= rs) & (kr < rs + NA_WIN_H))
        dr_l.append(np.clip(kr - r + NA_WIN_H - 1, 0, 2 * NA_WIN_H - 2))
    dr = np.stack(dr_l)
    rvalid = np.stack(rv_l)
    c = np.arange(GRID_W)[:, None]
    kc = np.arange(GRID_W)[None, :]
    cs = np.clip(c - NA_WIN_W // 2, 0, GRID_W - NA_WIN_W)
    cvalid = (kc >= cs) & (kc < cs + NA_WIN_W)
    dc = np.clip(kc - c + NA_WIN_W - 1, 0, 2 * NA_WIN_W - 2)
    g = rpb[:, dr[:, :, None, :, None], dc[None, None, :, None, :]]
    valid = rvalid[:, :, None, :, None] & cvalid[None, None, :, None, :]
    g = jnp.where(valid[None], g, NEG_INF)
    return g.transpose(1, 0, 2, 3, 4, 5).reshape(3, NA_HEADS, NA_TQ, NA_TK)


def _mla_kernel(q_ref, k_ref, v_ref, o_ref):
    lo_half = lax.broadcasted_iota(jnp.int32, (1, LANES), 1) < MLA_V
    v = v_ref[0]
    outs = []
    for hh in range(2):
        sl = slice(hh * LANES, (hh + 1) * LANES)
        s = _nt_dot(q_ref[0, :, sl], k_ref[0, :, sl])
        m = jnp.max(s, axis=-1, keepdims=True)
        e = jnp.exp(s - m)
        l = jnp.sum(e, axis=-1, keepdims=True)
        outs.append(jnp.dot(e.astype(BF16), v, preferred_element_type=F32) / l)
    o_ref[0] = jnp.where(lo_half, outs[0], outs[1])


def _mla_call(mq, mk, mv):
    b = mq.shape[0]
    pairs = MLA_HEADS // 2
    return pl.pallas_call(
        _mla_kernel,
        grid=(b, pairs, SEQ // MLA_TQ),
        in_specs=[pl.BlockSpec((1, MLA_TQ, 2 * LANES), lambda bi, p, i: (bi, i, p)),
                  pl.BlockSpec((1, SEQ, 2 * LANES), lambda bi, p, i: (bi, 0, p)),
                  pl.BlockSpec((1, SEQ, LANES), lambda bi, p, i: (bi, 0, p))],
        out_specs=pl.BlockSpec((1, MLA_TQ, LANES), lambda bi, p, i: (bi, i, p)),
        out_shape=jax.ShapeDtypeStruct((b, SEQ, B_W), F32),
        compiler_params=_cparams(("arbitrary", "arbitrary", "arbitrary")),
        name="latent_attn",
    )(mq, mk, mv)


def _swa_key_start(i):
    return jnp.clip(SWA_TQ * i - SWA_WINDOW, 0, SEQ - SWA_TK)


def _swa_kernel(sink_ref, q_ref, k_ref, v_ref, o_ref):
    i = pl.program_id(1)
    ws = pl.multiple_of(_swa_key_start(i), LANES)
    kwin = k_ref[0, pl.ds(ws, SWA_TK), :]
    vwin = v_ref[0, pl.ds(ws, SWA_TK), :]
    qpos = SWA_TQ * i + lax.broadcasted_iota(jnp.int32, (SWA_TQ, 1), 0)
    kpos = ws + lax.broadcasted_iota(jnp.int32, (1, SWA_TK), 1)
    valid = jnp.abs(qpos - kpos) <= SWA_WINDOW
    lo_half = lax.broadcasted_iota(jnp.int32, (1, LANES), 1) < HEAD_DIM
    for g in range(SWA_GROUP):
        outs = []
        for kvh in range(SWA_KV_HEADS):
            grp = 2 * g + kvh
            s = _nt_dot(q_ref[0, :, grp * LANES:(grp + 1) * LANES], kwin)
            s = jnp.where(valid, s, NEG_INF)
            sink = sink_ref[kvh * SWA_GROUP + g]
            m = jnp.maximum(jnp.max(s, axis=-1, keepdims=True), sink)
            e = jnp.exp(s - m)
            l = jnp.sum(e, axis=-1, keepdims=True) + jnp.exp(sink - m)
            outs.append(jnp.dot(e.astype(BF16), vwin, preferred_element_type=F32) / l)
        o_ref[0, :, g * LANES:(g + 1) * LANES] = jnp.where(lo_half, outs[0], outs[1])


def _swa_call(sink, sq, sk, sv):
    b = sq.shape[0]
    return pl.pallas_call(
        _swa_kernel,
        grid=(b, SEQ // SWA_TQ),
        in_specs=[pl.BlockSpec(memory_space=pltpu.SMEM),
                  pl.BlockSpec((1, SWA_TQ, 2 * C_W), lambda bi, i: (bi, i, 0)),
                  pl.BlockSpec((1, SEQ, LANES), lambda bi, i: (bi, 0, 0)),
                  pl.BlockSpec((1, SEQ, LANES), lambda bi, i: (bi, 0, 0))],
        out_specs=pl.BlockSpec((1, SWA_TQ, C_W), lambda bi, i: (bi, i, 0)),
        out_shape=jax.ShapeDtypeStruct((b, SEQ, C_W), F32),
        compiler_params=_cparams(("arbitrary", "arbitrary")),
        name="window_gqa",
    )(sink, sq, sk, sv)


def _out_kernel(oa_ref, ob_ref, oc_ref, h_ref, ga_ref, gb_ref, gc_ref, wa_ref, wb_ref, wc_ref,
                gf_ref, wr_ref, hm_ref, xn_ref, aff_ref):
    acc = h_ref[...]
    for o_ref, g_ref, w_ref in ((oa_ref, ga_ref, wa_ref), (ob_ref, gb_ref, wb_ref), (oc_ref, gc_ref, wc_ref)):
        acc = acc + jnp.dot(_rms(o_ref[...], g_ref[...]).astype(BF16), w_ref[...], preferred_element_type=F32)
    hm_ref[...] = acc
    xn = _rms(acc, gf_ref[...])
    xn_ref[...] = xn
    logits = lax.dot_general(wr_ref[...], xn, (((1,), (1,)), ((), ())),
                             precision=lax.Precision.HIGHEST, preferred_element_type=F32)
    m = jnp.max(logits, axis=0, keepdims=True)
    e = jnp.exp(logits - m)
    aff_ref[0] = e / jnp.sum(e, axis=0, keepdims=True)


def _out_call(oa, ob, oc, h2, ga, gb, gc, wa, wb, wc, gf, wr_t):
    n = h2.shape[0]
    tm = ROW_TILE
    seq_tiles = SEQ // tm
    row = lambda w: pl.BlockSpec((tm, w), lambda i: (i, 0))
    full = lambda a: pl.BlockSpec(a.shape, lambda i: (0,) * a.ndim)
    return pl.pallas_call(
        _out_kernel,
        grid=(n // tm,),
        in_specs=[row(A_W), row(B_W), row(C_W), row(D_MODEL), full(ga), full(gb), full(gc),
                  full(wa), full(wb), full(wc), full(gf), full(wr_t)],
        out_specs=[row(D_MODEL), row(D_MODEL),
                   pl.BlockSpec((1, N_EXPERTS, tm), lambda i: (i // seq_tiles, 0, i % seq_tiles))],
        out_shape=[jax.ShapeDtypeStruct((n, D_MODEL), F32), jax.ShapeDtypeStruct((n, D_MODEL), F32),
                   jax.ShapeDtypeStruct((n // SEQ, N_EXPERTS, SEQ), F32)],
        compiler_params=_cparams(("arbitrary",)),
        name="out_proj_router",
    )(oa, ob, oc, h2, ga, gb, gc, wa, wb, wc, gf, wr_t)


def _topk_kernel(aff_ref, idx_ref, gate_ref, posm_ref):
    rows, seq = aff_ref.shape
    tiles = seq // LANES
    a = aff_ref[...]
    bits = lax.bitcast_convert_type(a, jnp.int32)
    key = bits ^ (lax.shift_right_arithmetic(bits, jnp.int32(31)) & jnp.int32(0x7FFFFFFF))
    int_min = jnp.int32(-2 ** 31)

    t_u = jnp.zeros((rows, 1), jnp.int32)
    for bit in range(31, -1, -1):
        step = int_min if bit == 31 else jnp.int32(1 << bit)
        cand_u = t_u | step
        cnt = jnp.sum(jnp.where(key >= (cand_u ^ int_min), 1.0, 0.0), axis=1, keepdims=True)
        t_u = jnp.where(cnt >= CAP, cand_u, t_u)
    thr = t_u ^ int_min
    gt = key > thr
    eq = key == thr
    need = CAP - jnp.sum(jnp.where(gt, 1.0, 0.0), axis=1, keepdims=True)

    tri = jnp.where(lax.broadcasted_iota(jnp.int32, (LANES, LANES), 0)
                    <= lax.broadcasted_iota(jnp.int32, (LANES, LANES), 1), 1.0, 0.0).astype(BF16)

    def prefix_incl(flags_f32, t, carry):
        blk = flags_f32[:, t * LANES:(t + 1) * LANES]
        inc = jnp.dot(blk.astype(BF16), tri, preferred_element_type=F32) + carry
        return blk, inc, inc[:, LANES - 1:LANES]

    eq_f = jnp.where(eq, 1.0, 0.0)
    gt_f = jnp.where(gt, 1.0, 0.0)
    carry_eq = jnp.zeros((rows, 1), F32)
    carry_sel = jnp.zeros((rows, 1), F32)
    for t in range(tiles):
        eq_blk, eq_inc, carry_eq = prefix_incl(eq_f, t, carry_eq)
        sel_blk = jnp.maximum(gt_f[:, t * LANES:(t + 1) * LANES],
                              jnp.where(eq_inc <= need, eq_blk, 0.0))
        sel_inc = jnp.dot(sel_blk.astype(BF16), tri, preferred_element_type=F32) + carry_sel
        carry_sel = sel_inc[:, LANES - 1:LANES]
        posm_ref[:, t * LANES:(t + 1) * LANES] = jnp.where(sel_blk > 0.0, sel_inc - 1.0, -1.0)

    slot = lax.broadcasted_iota(jnp.int32, (CAP, LANES), 0).astype(F32)
    sub = lax.broadcasted_iota(jnp.int32, (16, LANES), 0)
    sub8 = lax.broadcasted_iota(jnp.int32, (8, CAP), 0)
    lane =lax.broadcasted_iota(jnp.int32, (1, LANES), 1)

    def tile_body(t, accs, r8):
        c0 = pl.multiple_of(t * LANES, LANES)
        pos8 = posm_ref[pl.ds(r8, 8), pl.ds(c0, LANES)]
        aff8 = aff_ref[pl.ds(r8, 8), pl.ds(c0, LANES)]
        tok = lane + t * LANES
        hi = lax.shift_right_logical(tok, 6).astype(F32)
        lo = (tok & 63).astype(F32)
        out = []
        for k in range(8):
            onehot = jnp.where(pos8[k:k + 1, :] == slot, 1.0, 0.0).astype(BF16)
            av = aff8[k:k + 1, :]
            g0 = av.astype(BF16).astype(F32)
            r1 = av - g0
            g1 = r1.astype(BF16).astype(F32)
            g2 = r1 - g1
            data = jnp.zeros((16, LANES), F32)
            for r, piece in enumerate((hi, lo, g0, g1, g2)):
                data = jnp.where(sub == r, jnp.broadcast_to(piece, (16, LANES)), data)
            out.append(accs[k] + _nt_dot(data.astype(BF16), onehot))
        return tuple(out)

    def row_body(i8, carry):
        r8 = pl.multiple_of(i8 * 8, 8)
        accs = lax.fori_loop(0, tiles, functools.partial(tile_body, r8=r8),
                             tuple(jnp.zeros((16, CAP), F32) for _ in range(8)))
        idx8 = jnp.zeros((8, CAP), F32)
        gate8 = jnp.zeros((8, CAP), F32)
        for k in range(8):
            acc = accs[k]
            idx8 = jnp.where(sub8 == k, jnp.broadcast_to(acc[0:1] * 64.0 + acc[1:2], (8, CAP)), idx8)
            gate8 = jnp.where(sub8 == k, jnp.broadcast_to(acc[2:3] + acc[3:4] + acc[4:5], (8, CAP)), gate8)
        idx_ref[pl.ds(r8, 8), :] = idx8.astype(jnp.int32)
        gate_ref[pl.ds(r8, 8), :] = gate8
        return carry

    lax.fori_loop(0, rows // 8, row_body, 0)


def _topk_call(aff2):
    rows, seq = aff2.shape
    return pl.pallas_call(
        _topk_kernel,
        out_shape=[jax.ShapeDtypeStruct((rows, CAP), jnp.int32), jax.ShapeDtypeStruct((rows, CAP), F32)],
        scratch_shapes=[pltpu.VMEM((rows, seq), F32)],
        compiler_params=pltpu.CompilerParams(vmem_limit_bytes=VMEM_LIMIT),
        name="expert_choice_topk",
    )(aff2)


def _ffn_kernel(idx_ref, x_hbm, wg_ref, wu_ref, wd_ref, y_ref, xs_ref, sem_ref):
    b = pl.program_id(0)
    e = pl.program_id(1)
    steps = pl.num_programs(0) * N_EXPERTS
    t = b * N_EXPERTS + e

    def issue(step, slot):
        base = step * CAP
        row0 = (step // N_EXPERTS) * SEQ

        def body(c, carry):
            tok = idx_ref[base + c]
            pltpu.make_async_copy(x_hbm.at[pl.ds(row0 + tok, 1), :], xs_ref.at[slot, pl.ds(c, 1), :],
                                  sem_ref.at[slot]).start()
            return carry

        lax.fori_loop(0, CAP, body, 0, unroll=8)

    @pl.when(t == 0)
    def _():
        issue(t, 0)

    @pl.when(t + 1 < steps)
    def _():
        issue(t + 1, (t + 1) % 2)

    slot = t % 2
    pltpu.make_async_copy(x_hbm.at[pl.ds(0, CAP), :], xs_ref.at[slot], sem_ref.at[slot]).wait()
    xs = xs_ref[slot].astype(BF16)
    gate = jnp.dot(xs, wg_ref[0], preferred_element_type=F32)
    up = jnp.dot(xs, wu_ref[0], preferred_element_type=F32)
    hid = (gate * (1.0 / (1.0 + jnp.exp(-gate))) * up).astype(BF16)
    y_ref[0, 0] = jnp.dot(hid, wd_ref[0], preferred_element_type=F32)


def _ffn_call(idx_flat, xn2, wg, wu, wd):
    b = xn2.shape[0] // SEQ
    wspec = pl.BlockSpec((1, D_MODEL, D_MODEL), lambda bi, e, idx: (e, 0, 0))
    return pl.pallas_call(
        _ffn_kernel,
        grid_spec=pltpu.PrefetchScalarGridSpec(
            num_scalar_prefetch=1,
            grid=(b, N_EXPERTS),
            in_specs=[pl.BlockSpec(memory_space=pl.ANY), wspec, wspec, wspec],
            out_specs=pl.BlockSpec((1, 1, CAP, D_MODEL), lambda bi, e, idx: (bi, e, 0, 0)),
            scratch_shapes=[pltpu.VMEM((2, CAP, D_MODEL), F32), pltpu.SemaphoreType.DMA((2,))],
        ),
        out_shape=jax.ShapeDtypeStruct((b, N_EXPERTS, CAP, D_MODEL), F32),
        compiler_params=_cparams(("arbitrary", "arbitrary")),
        name="expert_ffn",
    )(idx_flat, xn2, wg, wu, wd)


def _combine_kernel(idx_ref, gate_ref, y_ref, h_hbm, o_hbm, acc_ref, sem_ref):
    b = pl.program_id(0)
    e = pl.program_id(1)

    @pl.when(e == 0)
    def _():
        cp = pltpu.make_async_copy(h_hbm.at[b], acc_ref, sem_ref.at[0])
        cp.start()
        cp.wait()

    base = (b * N_EXPERTS + e) * CAP

    def body(c, carry):
        tok = idx_ref[base + c]
        g = gate_ref[base + c]
        acc_ref[pl.ds(tok, 1), :] = acc_ref[pl.ds(tok, 1), :] + y_ref[0, 0, pl.ds(c, 1), :] * g
        return carry

    lax.fori_loop(0, CAP, body, 0, unroll=8)

    @pl.when(e == N_EXPERTS - 1)
    def _():
        cp = pltpu.make_async_copy(acc_ref, o_hbm.at[b], sem_ref.at[0])
        cp.start()
        cp.wait()


def _combine_call(idx_flat, gate_flat, y, h3):
    b = h3.shape[0]
    return pl.pallas_call(
        _combine_kernel,
        grid_spec=pltpu.PrefetchScalarGridSpec(
            num_scalar_prefetch=2,
            grid=(b, N_EXPERTS),
            in_specs=[pl.BlockSpec((1, 1, CAP, D_MODEL), lambda bi, e, idx, gt: (bi, e, 0, 0)),
                      pl.BlockSpec(memory_space=pl.ANY)],
            out_specs=pl.BlockSpec(memory_space=pl.ANY),
            scratch_shapes=[pltpu.VMEM((SEQ, D_MODEL), F32), pltpu.SemaphoreType.DMA((1,))],
        ),
        out_shape=jax.ShapeDtypeStruct(h3.shape, F32),
        compiler_params=_cparams(("arbitrary", "arbitrary")),
        name="expert_combine",
    )(idx_flat, gate_flat, y, h3)


def _norm_kernel(x_ref, g_ref, o_ref):
    o_ref[...] = _rms(x_ref[...], g_ref[...])


def _norm_call(x2, gain):
    n = x2.shape[0]
    tm = ROW_TILE
    return pl.pallas_call(
        _norm_kernel,
        grid=(n // tm,),
        in_specs=[pl.BlockSpec((tm, D_MODEL), lambda i: (i, 0)), pl.BlockSpec((1, D_MODEL), lambda i: (0, 0))],
        out_specs=pl.BlockSpec((tm, D_MODEL), lambda i: (i, 0)),
        out_shape=jax.ShapeDtypeStruct((n, D_MODEL), F32),
        compiler_params=_cparams(("arbitrary",)),
        name="final_norm",
    )(x2, gain)


def _rope_table(dim, lead):
    half = dim // 2
    inv = 1.0 / (ROPE_THETA ** (jnp.arange(0, dim, 2, dtype=F32) / dim))
    ang = jnp.arange(SEQ, dtype=F32)[:, None] * inv[None, :]
    cos, sin = jnp.cos(ang), jnp.sin(ang)
    zero = jnp.zeros_like(sin)
    if lead:
        tail = jnp.zeros((SEQ, LANES - lead - dim), F32)
        ones = jnp.ones((SEQ, lead), F32)
        zl = jnp.zeros((SEQ, lead), F32)
        c = jnp.concatenate([ones, cos, cos, tail], axis=1)
        sa = jnp.concatenate([zl, zero, sin, tail], axis=1)
        sb = jnp.concatenate([zl, -sin, zero, tail], axis=1)
    else:
        reps = LANES // dim
        c = jnp.concatenate([cos, cos] * reps, axis=1)
        sa = jnp.concatenate([zero, sin] * reps, axis=1)
        sb = jnp.concatenate([-sin, zero] * reps, axis=1)
    return jnp.concatenate([c, sa, sb], axis=1)


def _swa_head_perm():
    return [kvh * SWA_GROUP + g for g in range(SWA_GROUP) for kvh in range(SWA_KV_HEADS)]


def _permute_w_in(w):
    offs = np.cumsum([0, A_W, A_W, A_W, MLA_Q_LORA, MLA_KV_LORA, MLA_ROPE, C_W, 2 * HEAD_DIM, 2 * HEAD_DIM])
    a_q, a_k, a_v, b_cq, b_ckv, b_kr, c_q, c_k, c_v = [w[:, offs[i]:offs[i + 1]] for i in range(9)]
    c_q = jnp.concatenate([c_q[:, h * HEAD_DIM:(h + 1) * HEAD_DIM] for h in _swa_head_perm()], axis=1)
    zeros = lambda n: jnp.zeros((w.shape[0], n), w.dtype)
    kr = jnp.concatenate([zeros(MLA_NOPE), b_kr, zeros(LANES - MLA_NOPE - MLA_ROPE)], axis=1)
    return jnp.concatenate([a_q, a_k, a_v, b_cq, b_ckv, c_q, c_k, c_v, kr], axis=1).astype(BF16)


def _permute_mla(w_uq, w_ukv):
    zq = jnp.zeros((MLA_Q_LORA, LANES - MLA_NOPE - MLA_ROPE), w_uq.dtype)
    zk = jnp.zeros((MLA_KV_LORA, LANES - MLA_NOPE), w_ukv.dtype)
    dq = MLA_NOPE + MLA_ROPE
    dkv = MLA_NOPE + MLA_V
    wq = jnp.concatenate([jnp.concatenate([w_uq[:, h * dq:(h + 1) * dq], zq], axis=1)
                          for h in range(MLA_HEADS)], axis=1)
    wk = jnp.concatenate([jnp.concatenate([w_ukv[:, h * dkv:h * dkv + MLA_NOPE], zk], axis=1)
                          for h in range(MLA_HEADS)], axis=1)
    wv = jnp.concatenate([w_ukv[:, h * dkv + MLA_NOPE:(h + 1) * dkv] for h in range(MLA_HEADS)], axis=1)
    return wq.astype(BF16), jnp.concatenate([wk, wv], axis=1).astype(BF16)


def kernel(x, attn_norm, w_in, na_rpb, mla_q_norm, mla_w_uq, mla_kv_norm, mla_w_ukv, swa_sink, group_norm,
           w_out, ffn_norm, w_router, w_gate, w_up, w_down, final_norm):
    bsz, seq, d = x.shape
    assert (seq, d) == (SEQ, D_MODEL)
    n = bsz * seq
    depth = w_in.shape[0]
    rope_s = _rope_table(HEAD_DIM, 0)
    rope_m = _rope_table(MLA_ROPE, MLA_NOPE)
    c_perm = np.concatenate([np.arange(h * HEAD_DIM, (h + 1) * HEAD_DIM) for h in _swa_head_perm()])

    h2 = x.reshape(n, d)
    for l in range(depth):
        wq, wkv = _permute_mla(mla_w_uq[l], mla_w_ukv[l])
        aq, ak, av, mq, mk, mv, sq, sk, sv = _proj_call(
            h2, attn_norm[l][None], _permute_w_in(w_in[l]), mla_q_norm[l][None], wq, mla_kv_norm[l][None], wkv,
            rope_s, rope_m)
        r3 = lambda a: a.reshape(bsz, seq, a.shape[-1])
        o_a = _na_call(r3(aq), r3(ak), r3(av), _na_bias_tables(na_rpb[l]))
        o_b = _mla_call(r3(mq), r3(mk), r3(mv))
        o_c = _swa_call(swa_sink[l], r3(sq), r3(sk), r3(sv))

        gn = group_norm[l]
        wo = w_out[l]
        gc = gn[A_W + B_W:][c_perm]
        wc = wo[A_W + B_W:][c_perm]
        hm, xn2, aff_t = _out_call(
            o_a.reshape(n, A_W), o_b.reshape(n, B_W), o_c.reshape(n, C_W), h2,
            gn[None, :A_W], gn[None, A_W:A_W + B_W], gc[None],
            wo[:A_W].astype(BF16), wo[A_W:A_W + B_W].astype(BF16), wc.astype(BF16),
            ffn_norm[l][None], w_router[l].T)

        idx, gate = _topk_call(aff_t.reshape(bsz * N_EXPERTS, seq))
        idx_flat = idx.reshape(-1)
        y = _ffn_call(idx_flat, xn2, w_gate[l].astype(BF16), w_up[l].astype(BF16), w_down[l].astype(BF16))
        h2 = _combine_call(idx_flat, gate.reshape(-1), y, hm.reshape(bsz, seq, d)).reshape(n, d)
    return _norm_call(h2, final_norm[None]).reshape(bsz, seq, d)
```

```python
import functools

import jax
import jax.numpy as jnp
import numpy as np
from jax import lax
from jax.experimental import pallas as pl
from jax.experimental.pallas import tpu as pltpu

F32 = jnp.float32
BF16 = jnp.bfloat16

D_MODEL = 1024
SEQ = 4096
HEAD_DIM = 64
GRID_W = 64
NA_HEADS = 4
NA_WIN_H = 8
NA_WIN_W = 16
MLA_HEADS = 6
MLA_Q_LORA = 256
MLA_KV_LORA = 128
MLA_NOPE = 64
MLA_ROPE = 32
MLA_V = 64
SWA_HEADS = 6
SWA_KV_HEADS = 2
SWA_GROUP = SWA_HEADS // SWA_KV_HEADS
SWA_WINDOW = 128
ROPE_THETA = 10000.0
N_EXPERTS = 16
EC_CAPACITY = 2
CAP = EC_CAPACITY * SEQ // N_EXPERTS
RMS_EPS = 1e-6
NEG_INF = -1e30

A_W = NA_HEADS * HEAD_DIM
B_W = MLA_HEADS * MLA_V
C_W = SWA_HEADS * HEAD_DIM

LANES = 128
ROW_TILE = 512
VMEM_LIMIT = 56 * 1024 * 1024

P_AQ, P_AK, P_AV = 0, 256, 512
P_CQ, P_CKV = 768, 1024
P_SQ, P_SK, P_SV = 1152, 1536, 1664
P_KR = 1792
P_COLS = 1920

NA_QROWS = 8
NA_KROWS = 16
NA_TQ = NA_QROWS * GRID_W
NA_TK = NA_KROWS * GRID_W
MLA_TQ = 256
SWA_TQ = 512
SWA_TK = SWA_TQ + 2 * SWA_WINDOW


def _cparams(sem):
    return pltpu.CompilerParams(dimension_semantics=sem, vmem_limit_bytes=VMEM_LIMIT)


def _rms(x, gain):
    ms = jnp.mean(x * x, axis=-1, keepdims=True)
    return x * lax.rsqrt(ms + RMS_EPS) * gain


def _nt_dot(a, b):
    return lax.dot_general(a, b, (((1,), (1,)), ((), ())), preferred_element_type=F32)


def _proj_kernel(x_ref, gain_ref, win_ref, qn_ref, wq_ref, kvn_ref, wkv_ref, rs_ref, rm_ref,
                 aq_ref, ak_ref, av_ref, mq_ref, mk_ref, mv_ref, sq_ref, sk_ref, sv_ref):
    xn = _rms(x_ref[...], gain_ref[...]).astype(BF16)
    proj = jnp.dot(xn, win_ref[...], preferred_element_type=F32)
    lane = lax.broadcasted_iota(jnp.int32, (1, LANES), 1)
    lo_half = lane < HEAD_DIM

    def rope(grp, tab_ref, half):
        c = tab_ref[:, 0:LANES]
        sa = tab_ref[:, LANES:2 * LANES]
        sb = tab_ref[:, 2 * LANES:3 * LANES]
        return grp * c + pltpu.roll(grp, half, 1) * sa + pltpu.roll(grp, LANES - half, 1) * sb

    def split_heads(grp):
        zero = jnp.zeros_like(grp)
        return jnp.where(lo_half, grp, zero).astype(BF16), jnp.where(lo_half, zero, grp).astype(BF16)

    na_scale = HEAD_DIM ** -0.5
    for p in range(NA_HEADS // 2):
        grp = proj[:, P_AQ + p * LANES:P_AQ + (p + 1) * LANES] * na_scale
        q0, q1 = split_heads(grp)
        aq_ref[:, (2 * p) * LANES:(2 * p + 1) * LANES] = q0
        aq_ref[:, (2 * p + 1) * LANES:(2 * p + 2) * LANES] = q1
    ak_ref[...] = proj[:, P_AK:P_AK + A_W].astype(BF16)
    av_ref[...] = proj[:, P_AV:P_AV + A_W].astype(BF16)

    cq = _rms(proj[:, P_CQ:P_CQ + MLA_Q_LORA], qn_ref[...]).astype(BF16)
    q = jnp.dot(cq, wq_ref[...], preferred_element_type=F32) * ((MLA_NOPE + MLA_ROPE) ** -0.5)
    ckv = _rms(proj[:, P_CKV:P_CKV + MLA_KV_LORA], kvn_ref[...]).astype(BF16)
    kv = jnp.dot(ckv, wkv_ref[...], preferred_element_type=F32)
    kr = rope(proj[:, P_KR:P_KR + LANES], rm_ref, MLA_ROPE // 2)
    for h in range(MLA_HEADS):
        sl = slice(h * LANES, (h + 1) * LANES)
        mq_ref[:, sl] = rope(q[:, sl], rm_ref, MLA_ROPE // 2).astype(BF16)
        mk_ref[:, sl] = (kv[:, sl] + kr).astype(BF16)
    mv_ref[...] = kv[:, MLA_HEADS * LANES:MLA_HEADS * LANES + B_W].astype(BF16)

    swa_scale = HEAD_DIM ** -0.5
    for g in range(SWA_GROUP):
        grp = rope(proj[:, P_SQ + g * LANES:P_SQ + (g + 1) * LANES], rs_ref, HEAD_DIM // 2) * swa_scale
        q0, q1 = split_heads(grp)
        sq_ref[:, (2 * g) * LANES:(2 * g + 1) * LANES] = q0
        sq_ref[:, (2 * g + 1) * LANES:(2 * g + 2) * LANES] = q1
    sk_ref[...] = rope(proj[:, P_SK:P_SK + LANES], rs_ref, HEAD_DIM // 2).astype(BF16)
    sv_ref[...] = proj[:, P_SV:P_SV + LANES].astype(BF16)


def _proj_call(x2, gain, win, qn, wq, kvn, wkv, rope_s, rope_m):
    n = x2.shape[0]
    tm = ROW_TILE
    seq_tiles = SEQ // tm
    row = lambda w: pl.BlockSpec((tm, w), lambda i: (i, 0))
    full = lambda a: pl.BlockSpec(a.shape, lambda i: (0,) * a.ndim)
    pos = lambda w: pl.BlockSpec((tm, w), lambda i: (i % seq_tiles, 0))
    widths = (2 * A_W, A_W, A_W, MLA_HEADS * LANES, MLA_HEADS * LANES, B_W, 2 * C_W, LANES, LANES)
    return pl.pallas_call(
        _proj_kernel,
        grid=(n // tm,),
        in_specs=[row(D_MODEL), full(gain), full(win), full(qn), full(wq), full(kvn), full(wkv),
                  pos(3 * LANES), pos(3 * LANES)],
        out_specs=[row(w) for w in widths],
        out_shape=[jax.ShapeDtypeStruct((n, w), BF16) for w in widths],
        compiler_params=_cparams(("arbitrary",)),
        name="norm_in_proj",
    )(x2, gain, win, qn, wq, kvn, wkv, rope_s, rope_m)


def _na_key_start(j):
    rows = SEQ // GRID_W
    return jnp.clip(NA_QROWS * j - NA_WIN_H // 2, 0, rows - NA_KROWS)


def _na_kernel(q_ref, k_ref, v_ref, bias_ref, o_ref):
    j = pl.program_id(1)
    ks = pl.multiple_of(_na_key_start(j) * GRID_W, 256)
    kwin = k_ref[0, pl.ds(ks, NA_TK), :]
    vwin = v_ref[0, pl.ds(ks, NA_TK), :]
    lo_half = lax.broadcasted_iota(jnp.int32, (1, LANES), 1) < HEAD_DIM
    for p in range(NA_HEADS // 2):
        kp = kwin[:, p * LANES:(p + 1) * LANES]
        vp = vwin[:, p * LANES:(p + 1) * LANES]
        outs = []
        for hh in range(2):
            h = 2 * p + hh
            s = _nt_dot(q_ref[0, :, h * LANES:(h + 1) * LANES], kp) + bias_ref[0, h]
            m = jnp.max(s, axis=-1, keepdims=True)
            e = jnp.exp(s - m)
            l = jnp.sum(e, axis=-1, keepdims=True)
            outs.append(jnp.dot(e.astype(BF16), vp, preferred_element_type=F32) / l)
        o_ref[0, :, p * LANES:(p + 1) * LANES] = jnp.where(lo_half, outs[0], outs[1])


def _na_call(aq, ak, av, bias):
    b = aq.shape[0]
    nj = SEQ // NA_TQ

    def variant(bi, j):
        return (jnp.where(j == 0, 0, jnp.where(j == nj - 1, 2, 1)), 0, 0, 0)

    return pl.pallas_call(
        _na_kernel,
        grid=(b, nj),
        in_specs=[pl.BlockSpec((1, NA_TQ, 2 * A_W), lambda bi, j: (bi, j, 0)),
                  pl.BlockSpec((1, SEQ, A_W), lambda bi, j: (bi, 0, 0)),
                  pl.BlockSpec((1, SEQ, A_W), lambda bi, j: (bi, 0, 0)),
                  pl.BlockSpec((1, NA_HEADS, NA_TQ, NA_TK), variant)],
        out_specs=pl.BlockSpec((1, NA_TQ, A_W), lambda bi, j: (bi, j, 0)),
        out_shape=jax.ShapeDtypeStruct((b, SEQ, A_W), F32),
        compiler_params=_cparams(("arbitrary", "arbitrary")),
        name="neighbourhood_attn",
    )(aq, ak, av, bias)


def _na_bias_tables(rpb):
    rows = SEQ // GRID_W
    nj = SEQ // NA_TQ
    c = np.arange(GRID_W)[:, None]
    kc = np.arange(GRID_W)[None, :]
    cs = np.clip(c - NA_WIN_W // 2, 0, GRID_W - NA_WIN_W)
    cvalid = (kc >= cs) & (kc < cs + NA_WIN_W)
    pad = GRID_W - NA_WIN_W
    padded = jnp.pad(rpb, ((0, 0), (0, 0), (pad, pad)))
    toep = jnp.stack([padded[:, :, GRID_W - 1 - ci:2 * GRID_W - 1 - ci] for ci in range(GRID_W)], axis=2)
    toep = jnp.where(cvalid[None, None], toep, NEG_INF)
    masked = jnp.full((rpb.shape[0], GRID_W, GRID_W), NEG_INF, F32)
    variants = []
    for j in (0, 1, nj - 1):
        ks = int(np.clip(NA_QROWS * j - NA_WIN_H // 2, 0, rows - NA_KROWS))
        row_blocks = []
        for i in range(NA_QROWS):
            r = NA_QROWS * j + i
            rs = int(np.clip(r - NA_WIN_H // 2, 0, rows - NA_WIN_H))
            blocks = []
            for u in range(NA_KROWS):
                kr = ks + u
                blocks.append(toep[:, kr - r + NA_WIN_H - 1] if rs <= kr < rs + NA_WIN_H else masked)
            row_blocks.append(jnp.concatenate(blocks, axis=2))
        variants.append(jnp.concatenate(row_blocks, axis=1))
    return jnp.stack(variants)


def _mla_kernel(q_ref, k_ref, v_ref, o_ref):
    lo_half = lax.broadcasted_iota(jnp.int32, (1, LANES), 1) < MLA_V
    v = v_ref[0]
    outs = []
    for hh in range(2):
        sl = slice(hh * LANES, (hh + 1) * LANES)
        s = _nt_dot(q_ref[0, :, sl], k_ref[0, :, sl])
        m = jnp.max(s, axis=-1, keepdims=True)
        e = jnp.exp(s - m)
        l = jnp.sum(e, axis=-1, keepdims=True)
        outs.append(jnp.dot(e.astype(BF16), v, preferred_element_type=F32) / l)
    o_ref[0] = jnp.where(lo_half, outs[0], outs[1])


def _mla_call(mq, mk, mv):
    b = mq.shape[0]
    pairs = MLA_HEADS // 2
    return pl.pallas_call(
        _mla_kernel,
        grid=(b, pairs, SEQ // MLA_TQ),
        in_specs=[pl.BlockSpec((1, MLA_TQ, 2 * LANES), lambda bi, p, i: (bi, i, p)),
                  pl.BlockSpec((1, SEQ, 2 * LANES), lambda bi, p, i: (bi, 0, p)),
                  pl.BlockSpec((1, SEQ, LANES), lambda bi, p, i: (bi, 0, p))],
        out_specs=pl.BlockSpec((1, MLA_TQ, LANES), lambda bi, p, i: (bi, i, p)),
        out_shape=jax.ShapeDtypeStruct((b, SEQ, B_W), F32),
        compiler_params=_cparams(("arbitrary", "arbitrary", "arbitrary")),
        name="latent_attn",
    )(mq, mk, mv)


def _swa_key_start(i):
    return jnp.clip(SWA_TQ * i - SWA_WINDOW, 0, SEQ - SWA_TK)


def _swa_kernel(sink_ref, q_ref, k_ref, v_ref, o_ref):
    i = pl.program_id(1)
    ws = pl.multiple_of(_swa_key_start(i), LANES)
    kwin = k_ref[0, pl.ds(ws, SWA_TK), :]
    vwin = v_ref[0, pl.ds(ws, SWA_TK), :]
    qpos = SWA_TQ * i + lax.broadcasted_iota(jnp.int32, (SWA_TQ, 1), 0)
    kpos = ws + lax.broadcasted_iota(jnp.int32, (1, SWA_TK), 1)
    valid = jnp.abs(qpos - kpos) <= SWA_WINDOW
    lo_half = lax.broadcasted_iota(jnp.int32, (1, LANES), 1) < HEAD_DIM
    for g in range(SWA_GROUP):
        outs = []
        for kvh in range(SWA_KV_HEADS):
            grp = 2 * g + kvh
            s = _nt_dot(q_ref[0, :, grp * LANES:(grp + 1) * LANES], kwin)
            s = jnp.where(valid, s, NEG_INF)
            sink = sink_ref[kvh * SWA_GROUP + g]
            m = jnp.maximum(jnp.max(s, axis=-1, keepdims=True), sink)
            e = jnp.exp(s - m)
            l = jnp.sum(e, axis=-1, keepdims=True) + jnp.exp(sink - m)
            outs.append(jnp.dot(e.astype(BF16), vwin, preferred_element_type=F32) / l)
        o_ref[0, :, g * LANES:(g + 1) * LANES] = jnp.where(lo_half, outs[0], outs[1])


def _swa_call(sink, sq, sk, sv):
    b = sq.shape[0]
    return pl.pallas_call(
        _swa_kernel,
        grid=(b, SEQ // SWA_TQ),
        in_specs=[pl.BlockSpec(memory_space=pltpu.SMEM),
                  pl.BlockSpec((1, SWA_TQ, 2 * C_W), lambda bi, i: (bi, i, 0)),
                  pl.BlockSpec((1, SEQ, LANES), lambda bi, i: (bi, 0, 0)),
                  pl.BlockSpec((1, SEQ, LANES), lambda bi, i: (bi, 0, 0))],
        out_specs=pl.BlockSpec((1, SWA_TQ, C_W), lambda bi, i: (bi, i, 0)),
        out_shape=jax.ShapeDtypeStruct((b, SEQ, C_W), F32),
        compiler_params=_cparams(("arbitrary", "arbitrary")),
        name="window_gqa",
    )(sink, sq, sk, sv)


def _out_kernel(oa_ref, ob_ref, oc_ref, h_ref, ga_ref, gb_ref, gc_ref, wa_ref, wb_ref, wc_ref,
                gf_ref, wr_ref, hm_ref, xn_ref, aff_ref):
    acc = h_ref[...]
    for o_ref, g_ref, w_ref in ((oa_ref, ga_ref, wa_ref), (ob_ref, gb_ref, wb_ref), (oc_ref, gc_ref, wc_ref)):
        acc = acc + jnp.dot(_rms(o_ref[...], g_ref[...]).astype(BF16), w_ref[...], preferred_element_type=F32)
    hm_ref[...] = acc
    xn = _rms(acc, gf_ref[...])
    xn_ref[...] = xn
    logits = lax.dot_general(wr_ref[...], xn, (((1,), (1,)), ((), ())),
                             precision=lax.Precision.HIGHEST, preferred_element_type=F32)
    m = jnp.max(logits, axis=0, keepdims=True)
    e = jnp.exp(logits - m)
    aff_ref[0] = e / jnp.sum(e, axis=0, keepdims=True)


def _out_call(oa, ob, oc, h2, ga, gb, gc, wa, wb, wc, gf, wr_t):
    n = h2.shape[0]
    tm = ROW_TILE
    seq_tiles = SEQ // tm
    row = lambda w: pl.BlockSpec((tm, w), lambda i: (i, 0))
    full = lambda a: pl.BlockSpec(a.shape, lambda i: (0,) * a.ndim)
    return pl.pallas_call(
        _out_kernel,
        grid=(n // tm,),
        in_specs=[row(A_W), row(B_W), row(C_W), row(D_MODEL), full(ga), full(gb), full(gc),
                  full(wa), full(wb), full(wc), full(gf), full(wr_t)],
        out_specs=[row(D_MODEL), row(D_MODEL),
                   pl.BlockSpec((1, N_EXPERTS, tm), lambda i: (i // seq_tiles, 0, i % seq_tiles))],
        out_shape=[jax.ShapeDtypeStruct((n, D_MODEL), F32), jax.ShapeDtypeStruct((n, D_MODEL), F32),
                   jax.ShapeDtypeStruct((n // SEQ, N_EXPERTS, SEQ), F32)],
        compiler_params=_cparams(("arbitrary",)),
        name="out_proj_router",
    )(oa, ob, oc, h2, ga, gb, gc, wa, wb, wc, gf, wr_t)


def _topk_kernel(aff_ref, idx_ref, gate_ref, posm_ref):
    rows, seq = aff_ref.shape
    tiles = seq // LANES
    a = aff_ref[...]
    int_min = jnp.int32(-2 ** 31)

    def ordered_to_float(u):
        key = u ^ int_min
        bits = key ^ (lax.shift_right_arithmetic(key, jnp.int32(31)) & jnp.int32(0x7FFFFFFF))
        return lax.bitcast_convert_type(bits, F32)

    t_u = jnp.zeros((rows, 1), jnp.int32)
    for bit in range(31, -1, -1):
        step = int_min if bit == 31 else jnp.int32(1 << bit)
        cand_u = t_u | step
        cnt = jnp.sum(jnp.where(a >= ordered_to_float(cand_u), 1.0, 0.0), axis=1, keepdims=True)
        t_u = jnp.where(cnt >= CAP, cand_u, t_u)
    thr = ordered_to_float(t_u)
    gt = a > thr
    eq = a == thr
    need = CAP - jnp.sum(jnp.where(gt, 1.0, 0.0), axis=1, keepdims=True)

    tri = jnp.where(lax.broadcasted_iota(jnp.int32, (LANES, LANES), 0)
                    <= lax.broadcasted_iota(jnp.int32, (LANES, LANES), 1), 1.0, 0.0).astype(BF16)

    def prefix_incl(flags_f32, t, carry):
        blk = flags_f32[:, t * LANES:(t + 1) * LANES]
        inc = jnp.dot(blk.astype(BF16), tri, preferred_element_type=F32) + carry
        return blk, inc, inc[:, LANES - 1:LANES]

    eq_f = jnp.where(eq, 1.0, 0.0)
    gt_f = jnp.where(gt, 1.0, 0.0)
    carry_eq = jnp.zeros((rows, 1), F32)
    carry_sel = jnp.zeros((rows, 1), F32)
    for t in range(tiles):
        eq_blk, eq_inc, carry_eq = prefix_incl(eq_f, t, carry_eq)
        sel_blk = jnp.maximum(gt_f[:, t * LANES:(t + 1) * LANES],
                              jnp.where(eq_inc <= need, eq_blk, 0.0))
        sel_inc = jnp.dot(sel_blk.astype(BF16), tri, preferred_element_type=F32) + carry_sel
        carry_sel = sel_inc[:, LANES - 1:LANES]
        posm_ref[:, t * LANES:(t + 1) * LANES] = jnp.where(sel_blk > 0.0, sel_inc - 1.0, -1.0)

    slot = lax.broadcasted_iota(jnp.int32, (CAP, LANES), 0).astype(F32)
    sub = lax.broadcasted_iota(jnp.int32, (16, LANES), 0)
    sub8 = lax.broadcasted_iota(jnp.int32, (8, CAP), 0)
    lane =lax.broadcasted_iota(jnp.int32, (1, LANES), 1)

    def tile_body(t, accs, r8):
        c0 = pl.multiple_of(t * LANES, LANES)
        pos8 = posm_ref[pl.ds(r8, 8), pl.ds(c0, LANES)]
        aff8 = aff_ref[pl.ds(r8, 8), pl.ds(c0, LANES)]
        tok = lane + t * LANES
        hi = lax.shift_right_logical(tok, 6).astype(F32)
        lo = (tok & 63).astype(F32)
        out = []
        for k in range(8):
            onehot = jnp.where(pos8[k:k + 1, :] == slot, 1.0, 0.0).astype(BF16)
            av = aff8[k:k + 1, :]
            g0 = av.astype(BF16).astype(F32)
            r1 = av - g0
            g1 = r1.astype(BF16).astype(F32)
            g2 = r1 - g1
            data = jnp.zeros((16, LANES), F32)
            for r, piece in enumerate((hi, lo, g0, g1, g2)):
                data = jnp.where(sub == r, jnp.broadcast_to(piece, (16, LANES)), data)
            out.append(accs[k] + _nt_dot(data.astype(BF16), onehot))
        return tuple(out)

    def row_body(i8, carry):
        r8 = pl.multiple_of(i8 * 8, 8)
        accs = lax.fori_loop(0, tiles, functools.partial(tile_body, r8=r8),
                             tuple(jnp.zeros((16, CAP), F32) for _ in range(8)))
        idx8 = jnp.zeros((8, CAP), F32)
        gate8 = jnp.zeros((8, CAP), F32)
        for k in range(8):
            acc = accs[k]
            idx8 = jnp.where(sub8 == k, jnp.broadcast_to(acc[0:1] * 64.0 + acc[1:2], (8, CAP)), idx8)
            gate8 = jnp.where(sub8 == k, jnp.broadcast_to(acc[2:3] + acc[3:4] + acc[4:5], (8, CAP)), gate8)
        idx_ref[pl.ds(r8, 8), :] = idx8.astype(jnp.int32)
        gate_ref[pl.ds(r8, 8), :] = gate8
        return carry

    lax.fori_loop(0, rows // 8, row_body, 0)


def _topk_call(aff2):
    rows, seq = aff2.shape
    return pl.pallas_call(
        _topk_kernel,
        out_shape=[jax.ShapeDtypeStruct((rows, CAP), jnp.int32), jax.ShapeDtypeStruct((rows, CAP), F32)],
        scratch_shapes=[pltpu.VMEM((rows, seq), F32)],
        compiler_params=pltpu.CompilerParams(vmem_limit_bytes=VMEM_LIMIT),
        name="expert_choice_topk",
    )(aff2)


def _ffn_kernel(idx_ref, x_hbm, wg_ref, wu_ref, wd_ref, y_ref, xs_ref, sem_ref):
    b = pl.program_id(0)
    e = pl.program_id(1)
    steps = pl.num_programs(0) * N_EXPERTS
    t = b * N_EXPERTS + e

    def issue(step, slot):
        base = step * CAP
        row0 = (step // N_EXPERTS) * SEQ

        def body(c, carry):
            tok = idx_ref[base + c]
            pltpu.make_async_copy(x_hbm.at[pl.ds(row0 + tok, 1), :], xs_ref.at[slot, pl.ds(c, 1), :],
                                  sem_ref.at[slot]).start()
            return carry

        lax.fori_loop(0, CAP, body, 0, unroll=8)

    @pl.when(t == 0)
    def _():
        issue(t, 0)

    @pl.when(t + 1 < steps)
    def _():
        issue(t + 1, (t + 1) % 2)

    slot = t % 2
    pltpu.make_async_copy(x_hbm.at[pl.ds(0, CAP), :], xs_ref.at[slot], sem_ref.at[slot]).wait()
    xs = xs_ref[slot].astype(BF16)
    gate = jnp.dot(xs, wg_ref[0], preferred_element_type=F32)
    up = jnp.dot(xs, wu_ref[0], preferred_element_type=F32)
    hid = (gate * (1.0 / (1.0 + jnp.exp(-gate))) * up).astype(BF16)
    y_ref[0, 0] = jnp.dot(hid, wd_ref[0], preferred_element_type=F32)


def _ffn_call(idx_flat, xn2, wg, wu, wd):
    b = xn2.shape[0] // SEQ
    wspec = pl.BlockSpec((1, D_MODEL, D_MODEL), lambda bi, e, idx: (e, 0, 0))
    return pl.pallas_call(
        _ffn_kernel,
        grid_spec=pltpu.PrefetchScalarGridSpec(
            num_scalar_prefetch=1,
            grid=(b, N_EXPERTS),
            in_specs=[pl.BlockSpec(memory_space=pl.ANY), wspec, wspec, wspec],
            out_specs=pl.BlockSpec((1, 1, CAP, D_MODEL), lambda bi, e, idx: (bi, e, 0, 0)),
            scratch_shapes=[pltpu.VMEM((2, CAP, D_MODEL), F32), pltpu.SemaphoreType.DMA((2,))],
        ),
        out_shape=jax.ShapeDtypeStruct((b, N_EXPERTS, CAP, D_MODEL), F32),
        compiler_params=_cparams(("arbitrary", "arbitrary")),
        name="expert_ffn",
    )(idx_flat, xn2, wg, wu, wd)


def _combine_kernel(idx_ref, gate_ref, y_ref, h_hbm, o_hbm, acc_ref, sem_ref):
    b = pl.program_id(0)
    e = pl.program_id(1)

    @pl.when(e == 0)
    def _():
        cp = pltpu.make_async_copy(h_hbm.at[b], acc_ref, sem_ref.at[0])
        cp.start()
        cp.wait()

    base = (b * N_EXPERTS + e) * CAP

    def body(c, carry):
        tok = idx_ref[base + c]
        g = gate_ref[base + c]
        acc_ref[pl.ds(tok, 1), :] = acc_ref[pl.ds(tok, 1), :] + y_ref[0, 0, pl.ds(c, 1), :] * g
        return carry

    lax.fori_loop(0, CAP, body, 0, unroll=8)

    @pl.when(e == N_EXPERTS - 1)
    def _():
        cp = pltpu.make_async_copy(acc_ref, o_hbm.at[b], sem_ref.at[0])
        cp.start()
        cp.wait()


def _combine_call(idx_flat, gate_flat, y, h3):
    b = h3.shape[0]
    return pl.pallas_call(
        _combine_kernel,
        grid_spec=pltpu.PrefetchScalarGridSpec(
            num_scalar_prefetch=2,
            grid=(b, N_EXPERTS),
            in_specs=[pl.BlockSpec((1, 1, CAP, D_MODEL), lambda bi, e, idx, gt: (bi, e, 0, 0)),
                      pl.BlockSpec(memory_space=pl.ANY)],
            out_specs=pl.BlockSpec(memory_space=pl.ANY),
            scratch_shapes=[pltpu.VMEM((SEQ, D_MODEL), F32), pltpu.SemaphoreType.DMA((1,))],
        ),
        out_shape=jax.ShapeDtypeStruct(h3.shape, F32),
        compiler_params=_cparams(("arbitrary", "arbitrary")),
        name="expert_combine",
    )(idx_flat, gate_flat, y, h3)


def _norm_kernel(x_ref, g_ref, o_ref):
    o_ref[...] = _rms(x_ref[...], g_ref[...])


def _norm_call(x2, gain):
    n = x2.shape[0]
    tm = ROW_TILE
    return pl.pallas_call(
        _norm_kernel,
        grid=(n // tm,),
        in_specs=[pl.BlockSpec((tm, D_MODEL), lambda i: (i, 0)), pl.BlockSpec((1, D_MODEL), lambda i: (0, 0))],
        out_specs=pl.BlockSpec((tm, D_MODEL), lambda i: (i, 0)),
        out_shape=jax.ShapeDtypeStruct((n, D_MODEL), F32),
        compiler_params=_cparams(("arbitrary",)),
        name="final_norm",
    )(x2, gain)


def _rope_table(dim, lead):
    half = dim // 2
    inv = 1.0 / (ROPE_THETA ** (jnp.arange(0, dim, 2, dtype=F32) / dim))
    ang = jnp.arange(SEQ, dtype=F32)[:, None] * inv[None, :]
    cos, sin = jnp.cos(ang), jnp.sin(ang)
    zero = jnp.zeros_like(sin)
    if lead:
        tail = jnp.zeros((SEQ, LANES - lead - dim), F32)
        ones = jnp.ones((SEQ, lead), F32)
        zl = jnp.zeros((SEQ, lead), F32)
        c = jnp.concatenate([ones, cos, cos, tail], axis=1)
        sa = jnp.concatenate([zl, zero, sin, tail], axis=1)
        sb = jnp.concatenate([zl, -sin, zero, tail], axis=1)
    else:
        reps = LANES // dim
        c = jnp.concatenate([cos, cos] * reps, axis=1)
        sa = jnp.concatenate([zero, sin] * reps, axis=1)
        sb = jnp.concatenate([-sin, zero] * reps, axis=1)
    return jnp.concatenate([c, sa, sb], axis=1)


def _swa_head_perm():
    return [kvh * SWA_GROUP + g for g in range(SWA_GROUP) for kvh in range(SWA_KV_HEADS)]


def _permute_w_in(w):
    offs = np.cumsum([0, A_W, A_W, A_W, MLA_Q_LORA, MLA_KV_LORA, MLA_ROPE, C_W, 2 * HEAD_DIM, 2 * HEAD_DIM])
    a_q, a_k, a_v, b_cq, b_ckv, b_kr, c_q, c_k, c_v = [w[:, offs[i]:offs[i + 1]] for i in range(9)]
    c_q = jnp.concatenate([c_q[:, h * HEAD_DIM:(h + 1) * HEAD_DIM] for h in _swa_head_perm()], axis=1)
    zeros = lambda n: jnp.zeros((w.shape[0], n), w.dtype)
    kr = jnp.concatenate([zeros(MLA_NOPE), b_kr, zeros(LANES - MLA_NOPE - MLA_ROPE)], axis=1)
    return jnp.concatenate([a_q, a_k, a_v, b_cq, b_ckv, c_q, c_k, c_v, kr], axis=1).astype(BF16)


def _permute_mla(w_uq, w_ukv):
    zq = jnp.zeros((MLA_Q_LORA, LANES - MLA_NOPE - MLA_ROPE), w_uq.dtype)
    zk = jnp.zeros((MLA_KV_LORA, LANES - MLA_NOPE), w_ukv.dtype)
    dq = MLA_NOPE + MLA_ROPE
    dkv = MLA_NOPE + MLA_V
    wq = jnp.concatenate([jnp.concatenate([w_uq[:, h * dq:(h + 1) * dq], zq], axis=1)
                          for h in range(MLA_HEADS)], axis=1)
    wk = jnp.concatenate([jnp.concatenate([w_ukv[:, h * dkv:h * dkv + MLA_NOPE], zk], axis=1)
                          for h in range(MLA_HEADS)], axis=1)
    wv = jnp.concatenate([w_ukv[:, h * dkv + MLA_NOPE:(h + 1) * dkv] for h in range(MLA_HEADS)], axis=1)
    return wq.astype(BF16), jnp.concatenate([wk, wv], axis=1).astype(BF16)


def kernel(x, attn_norm, w_in, na_rpb, mla_q_norm, mla_w_uq, mla_kv_norm, mla_w_ukv, swa_sink, group_norm,
           w_out, ffn_norm, w_router, w_gate, w_up, w_down, final_norm):
    bsz, seq, d = x.shape
    assert (seq, d) == (SEQ, D_MODEL)
    n = bsz * seq
    depth = w_in.shape[0]
    rope_s = _rope_table(HEAD_DIM, 0)
    rope_m = _rope_table(MLA_ROPE, MLA_NOPE)
    c_perm = np.concatenate([np.arange(h * HEAD_DIM, (h + 1) * HEAD_DIM) for h in _swa_head_perm()])

    h2 = x.reshape(n, d)
    for l in range(depth):
        wq, wkv = _permute_mla(mla_w_uq[l], mla_w_ukv[l])
        aq, ak, av, mq, mk, mv, sq, sk, sv = _proj_call(
            h2, attn_norm[l][None], _permute_w_in(w_in[l]), mla_q_norm[l][None], wq, mla_kv_norm[l][None], wkv,
            rope_s, rope_m)
        r3 = lambda a: a.reshape(bsz, seq, a.shape[-1])
        o_a = _na_call(r3(aq), r3(ak), r3(av), _na_bias_tables(na_rpb[l]))
        o_b = _mla_call(r3(mq), r3(mk), r3(mv))
        o_c = _swa_call(swa_sink[l], r3(sq), r3(sk), r3(sv))

        gn = group_norm[l]
        wo = w_out[l]
        gc = gn[A_W + B_W:][c_perm]
        wc = wo[A_W + B_W:][c_perm]
        hm, xn2, aff_t = _out_call(
            o_a.reshape(n, A_W), o_b.reshape(n, B_W), o_c.reshape(n, C_W), h2,
            gn[None, :A_W], gn[None, A_W:A_W + B_W], gc[None],
            wo[:A_W].astype(BF16), wo[A_W:A_W + B_W].astype(BF16), wc.astype(BF16),
            ffn_norm[l][None], w_router[l].T)

        idx, gate = _topk_call(aff_t.reshape(bsz * N_EXPERTS, seq))
        idx_flat = idx.reshape(-1)
        y = _ffn_call(idx_flat, xn2, w_gate[l].astype(BF16), w_up[l].astype(BF16), w_down[l].astype(BF16))
        h2 = _combine_call(idx_flat, gate.reshape(-1), y, hm.reshape(bsz, seq, d)).reshape(n, d)
    return _norm_call(h2, final_norm[None]).reshape(bsz, seq, d)
```

```python
import functools

import jax
import jax.numpy as jnp
import numpy as np
from jax import lax
from jax.experimental import pallas as pl
from jax.experimental.pallas import tpu as pltpu

F32 = jnp.float32
BF16 = jnp.bfloat16

D_MODEL = 1024
SEQ = 4096
HEAD_DIM = 64
GRID_W = 64
NA_HEADS = 4
NA_WIN_H = 8
NA_WIN_W = 16
MLA_HEADS = 6
MLA_Q_LORA = 256
MLA_KV_LORA = 128
MLA_NOPE = 64
MLA_ROPE = 32
MLA_V = 64
SWA_HEADS = 6
SWA_KV_HEADS = 2
SWA_GROUP = SWA_HEADS // SWA_KV_HEADS
SWA_WINDOW = 128
ROPE_THETA = 10000.0
N_EXPERTS = 16
EC_CAPACITY = 2
CAP = EC_CAPACITY * SEQ // N_EXPERTS
RMS_EPS = 1e-6
NEG_INF = -1e30

A_W = NA_HEADS * HEAD_DIM
B_W = MLA_HEADS * MLA_V
C_W = SWA_HEADS * HEAD_DIM

LANES = 128
TOKEN_SUB = D_MODEL // LANES
ROW_TILE = 512
VMEM_LIMIT = 56 * 1024 * 1024

P_AQ, P_AK, P_AV = 0, 256, 512
P_CQ, P_CKV = 768, 1024
P_SQ, P_SK, P_SV = 1152, 1536, 1664
P_KR = 1792
P_COLS = 1920

NA_QROWS = 8
NA_KROWS = 16
NA_TQ = NA_QROWS * GRID_W
NA_TK = NA_KROWS * GRID_W
MLA_TQ = 512
MLA_TK = 256
MLA_VT_ROWS = MLA_V + 16
LOG2_E = 1.4426950408889634
SWA_TQ = 512
SWA_TK = SWA_TQ + 2 * SWA_WINDOW


def _cparams(sem):
    return pltpu.CompilerParams(dimension_semantics=sem, vmem_limit_bytes=VMEM_LIMIT)


def _rms(x, gain):
    ms = jnp.mean(x * x, axis=-1, keepdims=True)
    return x * lax.rsqrt(ms + RMS_EPS) * gain


def _load_token_rows(ref, rows):
    return jnp.concatenate([ref[pl.ds(s, rows, stride=TOKEN_SUB), :] for s in range(TOKEN_SUB)], axis=1)


def _store_token_rows(ref, val):
    rows = val.shape[0]
    for s in range(TOKEN_SUB):
        ref[pl.ds(s, rows, stride=TOKEN_SUB), :] = val[:, s * LANES:(s + 1) * LANES]


def _nt_dot(a, b):
    return lax.dot_general(a, b, (((1,), (1,)), ((), ())), preferred_element_type=F32)


def _proj_kernel(x_ref, gain_ref, win_ref, qn_ref, wq_ref, kvn_ref, wkv_ref, rs_ref, rm_ref,
                 aq_ref, ak_ref, av_ref, mq_ref, mk_ref, mv_ref, sq_ref, sk_ref, sv_ref, *, tiled):
    x = _load_token_rows(x_ref, ROW_TILE) if tiled else x_ref[...]
    xn = _rms(x, gain_ref[...]).astype(BF16)
    proj = jnp.dot(xn, win_ref[...], preferred_element_type=F32)
    lane = lax.broadcasted_iota(jnp.int32, (1, LANES), 1)
    lo_half = lane < HEAD_DIM

    def rope(grp, tab_ref, half):
        c = tab_ref[:, 0:LANES]
        sa = tab_ref[:, LANES:2 * LANES]
        sb = tab_ref[:, 2 * LANES:3 * LANES]
        return grp * c + pltpu.roll(grp, half, 1) * sa + pltpu.roll(grp, LANES - half, 1) * sb

    def split_heads(grp):
        zero = jnp.zeros_like(grp)
        return jnp.where(lo_half, grp, zero).astype(BF16), jnp.where(lo_half, zero, grp).astype(BF16)

    na_scale = HEAD_DIM ** -0.5
    for p in range(NA_HEADS // 2):
        grp = proj[:, P_AQ + p * LANES:P_AQ + (p + 1) * LANES] * na_scale
        q0, q1 = split_heads(grp)
        aq_ref[:, (2 * p) * LANES:(2 * p + 1) * LANES] = q0
        aq_ref[:, (2 * p + 1) * LANES:(2 * p + 2) * LANES] = q1
    ak_ref[...] = proj[:, P_AK:P_AK + A_W].astype(BF16)
    av_ref[...] = proj[:, P_AV:P_AV + A_W].astype(BF16)

    cq = _rms(proj[:, P_CQ:P_CQ + MLA_Q_LORA], qn_ref[...]).astype(BF16)
    q = jnp.dot(cq, wq_ref[...], preferred_element_type=F32) * ((MLA_NOPE + MLA_ROPE) ** -0.5 * LOG2_E)
    ckv = _rms(proj[:, P_CKV:P_CKV + MLA_KV_LORA], kvn_ref[...]).astype(BF16)
    kv = jnp.dot(ckv, wkv_ref[...], preferred_element_type=F32)
    kr = rope(proj[:, P_KR:P_KR + LANES], rm_ref, MLA_ROPE // 2)
    for h in range(MLA_HEADS):
        sl = slice(h * LANES, (h + 1) * LANES)
        mq_ref[:, sl] = rope(q[:, sl], rm_ref, MLA_ROPE // 2).astype(BF16)
        mk_ref[:, sl] = (kv[:, sl] + kr).astype(BF16)
    vt = kv[:, MLA_HEADS * LANES:MLA_HEADS * LANES + B_W].T
    pad_rows = MLA_VT_ROWS - MLA_V
    ones_row = jnp.where(lax.broadcasted_iota(jnp.int32, (pad_rows, vt.shape[1]), 0) == 0, 1.0, 0.0)
    for h in range(MLA_HEADS):
        blk = jnp.concatenate([vt[h * MLA_V:(h + 1) * MLA_V], ones_row], axis=0)
        mv_ref[h * MLA_VT_ROWS:(h + 1) * MLA_VT_ROWS, :] = blk.astype(BF16)

    swa_scale = HEAD_DIM ** -0.5
    for g in range(SWA_GROUP):
        grp = rope(proj[:, P_SQ + g * LANES:P_SQ + (g + 1) * LANES], rs_ref, HEAD_DIM // 2) * swa_scale
        q0, q1 = split_heads(grp)
        sq_ref[:, (2 * g) * LANES:(2 * g + 1) * LANES] = q0
        sq_ref[:, (2 * g + 1) * LANES:(2 * g + 2) * LANES] = q1
    sk_ref[...] = rope(proj[:, P_SK:P_SK + LANES], rs_ref, HEAD_DIM // 2).astype(BF16)
    sv_ref[...] = proj[:, P_SV:P_SV + LANES].astype(BF16)


def _proj_call(x2, gain, win, qn, wq, kvn, wkv, rope_s, rope_m, tiled):
    tm = ROW_TILE
    n = x2.shape[0] // TOKEN_SUB if tiled else x2.shape[0]
    xspec = pl.BlockSpec((tm * TOKEN_SUB, LANES) if tiled else (tm, D_MODEL), lambda i: (i, 0))
    seq_tiles = SEQ // tm
    row = lambda w: pl.BlockSpec((tm, w), lambda i: (i, 0))
    full = lambda a: pl.BlockSpec(a.shape, lambda i: (0,) * a.ndim)
    pos = lambda w: pl.BlockSpec((tm, w), lambda i: (i % seq_tiles, 0))
    widths = (2 * A_W, A_W, A_W, MLA_HEADS * LANES, MLA_HEADS * LANES, B_W, 2 * C_W, LANES, LANES)
    mv_pos = widths.index(B_W)
    out_specs = [row(w) for w in widths]
    out_shape = [jax.ShapeDtypeStruct((n, w), BF16) for w in widths]
    out_specs[mv_pos] = pl.BlockSpec((MLA_HEADS * MLA_VT_ROWS, tm), lambda i: (0, i))
    out_shape[mv_pos] = jax.ShapeDtypeStruct((MLA_HEADS * MLA_VT_ROWS, n), BF16)
    return pl.pallas_call(
        functools.partial(_proj_kernel, tiled=tiled),
        grid=(n // tm,),
        in_specs=[xspec, full(gain), full(win), full(qn), full(wq), full(kvn), full(wkv),
                  pos(3 * LANES), pos(3 * LANES)],
        out_specs=out_specs,
        out_shape=out_shape,
        compiler_params=_cparams(("arbitrary",)),
        name="norm_in_proj",
    )(x2, gain, win, qn, wq, kvn, wkv, rope_s, rope_m)


def _na_key_start(j):
    rows = SEQ // GRID_W
    return jnp.clip(NA_QROWS * j - NA_WIN_H // 2, 0, rows - NA_KROWS)


def _na_kernel(q_ref, k_ref, v_ref, bias_ref, o_ref):
    j = pl.program_id(1)
    ks = pl.multiple_of(_na_key_start(j) * GRID_W, 256)
    kwin = k_ref[0, pl.ds(ks, NA_TK), :]
    vwin = v_ref[0, pl.ds(ks, NA_TK), :]
    lo_half = lax.broadcasted_iota(jnp.int32, (1, LANES), 1) < HEAD_DIM
    for p in range(NA_HEADS // 2):
        kp = kwin[:, p * LANES:(p + 1) * LANES]
        vp = vwin[:, p * LANES:(p + 1) * LANES]
        outs = []
        for hh in range(2):
            h = 2 * p + hh
            s = _nt_dot(q_ref[0, :, h * LANES:(h + 1) * LANES], kp) + bias_ref[0, h]
            m = jnp.max(s, axis=-1, keepdims=True)
            e = jnp.exp(s - m)
            l = jnp.sum(e, axis=-1, keepdims=True)
            outs.append(jnp.dot(e.astype(BF16), vp, preferred_element_type=F32) / l)
        o_ref[0, :, p * LANES:(p + 1) * LANES] = jnp.where(lo_half, outs[0], outs[1])


def _na_call(aq, ak, av, bias):
    b = aq.shape[0]
    nj = SEQ // NA_TQ

    def variant(bi, j):
        return (jnp.where(j == 0, 0, jnp.where(j == nj - 1, 2, 1)), 0, 0, 0)

    return pl.pallas_call(
        _na_kernel,
        grid=(b, nj),
        in_specs=[pl.BlockSpec((1, NA_TQ, 2 * A_W), lambda bi, j: (bi, j, 0)),
                  pl.BlockSpec((1, SEQ, A_W), lambda bi, j: (bi, 0, 0)),
                  pl.BlockSpec((1, SEQ, A_W), lambda bi, j: (bi, 0, 0)),
                  pl.BlockSpec((1, NA_HEADS, NA_TQ, NA_TK), variant)],
        out_specs=pl.BlockSpec((1, NA_TQ, A_W), lambda bi, j: (bi, j, 0)),
        out_shape=jax.ShapeDtypeStruct((b, SEQ, A_W), F32),
        compiler_params=_cparams(("arbitrary", "arbitrary")),
        name="neighbourhood_attn",
    )(aq, ak, av, bias)


def _na_bias_tables(rpb):
    rows = SEQ // GRID_W
    nj = SEQ // NA_TQ
    c = np.arange(GRID_W)[:, None]
    kc = np.arange(GRID_W)[None, :]
    cs = np.clip(c - NA_WIN_W // 2, 0, GRID_W - NA_WIN_W)
    cvalid = (kc >= cs) & (kc < cs + NA_WIN_W)
    pad = GRID_W - NA_WIN_W
    padded = jnp.pad(rpb, ((0, 0), (0, 0), (pad, pad)))
    toep = jnp.stack([padded[:, :, GRID_W - 1 - ci:2 * GRID_W - 1 - ci] for ci in range(GRID_W)], axis=2)
    toep = jnp.where(cvalid[None, None], toep, NEG_INF)
    masked = jnp.full((rpb.shape[0], GRID_W, GRID_W), NEG_INF, F32)
    variants = []
    for j in (0, 1, nj - 1):
        ks = int(np.clip(NA_QROWS * j - NA_WIN_H // 2, 0, rows - NA_KROWS))
        row_blocks = []
        for i in range(NA_QROWS):
            r = NA_QROWS * j + i
            rs = int(np.clip(r - NA_WIN_H // 2, 0, rows - NA_WIN_H))
            blocks = []
            for u in range(NA_KROWS):
                kr = ks + u
                blocks.append(toep[:, kr - r + NA_WIN_H - 1] if rs <= kr < rs + NA_WIN_H else masked)
            row_blocks.append(jnp.concatenate(blocks, axis=2))
        variants.append(jnp.concatenate(row_blocks, axis=1))
    return jnp.stack(variants)


def _mla_kernel(q_ref, k_ref, vt_ref, o_ref):
    heads = range(2)
    nchunks = SEQ // MLA_TK

    def scores(c, hh):
        sl = slice(hh * LANES, (hh + 1) * LANES)
        return _nt_dot(k_ref[0, c * MLA_TK:(c + 1) * MLA_TK, sl], q_ref[0, :, sl])

    m = [jnp.full((1, MLA_TQ), NEG_INF, F32) for _ in heads]
    acc = [jnp.zeros((MLA_VT_ROWS, MLA_TQ), F32) for _ in heads]
    st = [scores(0, hh) for hh in heads]
    for c in range(nchunks):
        st_next = [scores(c + 1, hh) for hh in heads] if c + 1 < nchunks else None
        for hh in heads:
            m_new = jnp.maximum(m[hh], jnp.max(st[hh], axis=0, keepdims=True))
            alpha = jnp.exp2(m[hh] - m_new)
            e = jnp.exp2(st[hh] - m_new).astype(BF16)
            vt = vt_ref[hh * MLA_VT_ROWS:(hh + 1) * MLA_VT_ROWS, c * MLA_TK:(c + 1) * MLA_TK]
            acc[hh] = alpha * acc[hh] + jnp.dot(vt, e, preferred_element_type=F32)
            m[hh] = m_new
        st = st_next
    o_ref[...] = jnp.concatenate([acc[hh][:MLA_V] / acc[hh][MLA_V:MLA_V + 1] for hh in heads], axis=0)


def _mla_call(mq, mk, mvt):
    b = mq.shape[0]
    pairs = MLA_HEADS // 2
    nq = SEQ // MLA_TQ
    return pl.pallas_call(
        _mla_kernel,
        grid=(b, pairs, nq),
        in_specs=[pl.BlockSpec((1, MLA_TQ, 2 * LANES), lambda bi, p, i: (bi, i, p)),
                  pl.BlockSpec((1, SEQ, 2 * LANES), lambda bi, p, i: (bi, 0, p)),
                  pl.BlockSpec((2 * MLA_VT_ROWS, SEQ), lambda bi, p, i: (p, bi))],
        out_specs=pl.BlockSpec((2 * MLA_V, MLA_TQ), lambda bi, p, i: (p, bi * nq + i)),
        out_shape=jax.ShapeDtypeStruct((B_W, b * SEQ), F32),
        compiler_params=_cparams(("arbitrary", "arbitrary", "arbitrary")),
        name="latent_attn",
    )(mq, mk, mvt)


def _swa_key_start(i):
    return jnp.clip(SWA_TQ * i - SWA_WINDOW, 0, SEQ - SWA_TK)


def _swa_kernel(sink_ref, q_ref, k_ref, v_ref, o_ref):
    i = pl.program_id(1)
    ws = pl.multiple_of(_swa_key_start(i), LANES)
    kwin = k_ref[0, pl.ds(ws, SWA_TK), :]
    vwin = v_ref[0, pl.ds(ws, SWA_TK), :]
    qpos = SWA_TQ * i + lax.broadcasted_iota(jnp.int32, (SWA_TQ, 1), 0)
    kpos = ws + lax.broadcasted_iota(jnp.int32, (1, SWA_TK), 1)
    valid = jnp.abs(qpos - kpos) <= SWA_WINDOW
    lo_half = lax.broadcasted_iota(jnp.int32, (1, LANES), 1) < HEAD_DIM
    for g in range(SWA_GROUP):
        outs = []
        for kvh in range(SWA_KV_HEADS):
            grp = 2 * g + kvh
            s = _nt_dot(q_ref[0, :, grp * LANES:(grp + 1) * LANES], kwin)
            s = jnp.where(valid, s, NEG_INF)
            sink = sink_ref[kvh * SWA_GROUP + g]
            m = jnp.maximum(jnp.max(s, axis=-1, keepdims=True), sink)
            e = jnp.exp(s - m)
            l = jnp.sum(e, axis=-1, keepdims=True) + jnp.exp(sink - m)
            outs.append(jnp.dot(e.astype(BF16), vwin, preferred_element_type=F32) / l)
        o_ref[0, :, g * LANES:(g + 1) * LANES] = jnp.where(lo_half, outs[0], outs[1])


def _swa_call(sink, sq, sk, sv):
    b = sq.shape[0]
    return pl.pallas_call(
        _swa_kernel,
        grid=(b, SEQ // SWA_TQ),
        in_specs=[pl.BlockSpec(memory_space=pltpu.SMEM),
                  pl.BlockSpec((1, SWA_TQ, 2 * C_W), lambda bi, i: (bi, i, 0)),
                  pl.BlockSpec((1, SEQ, LANES), lambda bi, i: (bi, 0, 0)),
                  pl.BlockSpec((1, SEQ, LANES), lambda bi, i: (bi, 0, 0))],
        out_specs=pl.BlockSpec((1, SWA_TQ, C_W), lambda bi, i: (bi, i, 0)),
        out_shape=jax.ShapeDtypeStruct((b, SEQ, C_W), F32),
        compiler_params=_cparams(("arbitrary", "arbitrary")),
        name="window_gqa",
    )(sink, sq, sk, sv)


def _out_kernel(oa_ref, ob_ref, oc_ref, h_ref, ga_ref, gb_ref, gc_ref, wa_ref, wb_ref, wc_ref,
                gf_ref, wr_ref, hm_ref, xn_ref, aff_ref, *, tiled):
    acc = _load_token_rows(h_ref, ROW_TILE) if tiled else h_ref[...]
    for o_ref, g_ref, w_ref in ((oa_ref, ga_ref, wa_ref), (oc_ref, gc_ref, wc_ref)):
        acc = acc + jnp.dot(_rms(o_ref[...], g_ref[...]).astype(BF16), w_ref[...], preferred_element_type=F32)
    ob = ob_ref[...]
    ms = jnp.mean(ob * ob, axis=0, keepdims=True)
    obn = (ob * lax.rsqrt(ms + RMS_EPS) * gb_ref[...]).astype(BF16)
    acc = acc + lax.dot_general(obn, wb_ref[...], (((0,), (0,)), ((), ())), preferred_element_type=F32)
    _store_token_rows(hm_ref, acc)
    xn = _rms(acc, gf_ref[...])
    _store_token_rows(xn_ref, xn)
    logits = lax.dot_general(wr_ref[...], xn, (((1,), (1,)), ((), ())),
                             precision=lax.Precision.HIGHEST, preferred_element_type=F32)
    m = jnp.max(logits, axis=0, keepdims=True)
    e = jnp.exp(logits - m)
    aff_ref[0] = e / jnp.sum(e, axis=0, keepdims=True)


def _out_call(oa, ob, oc, h2, ga, gb, gc, wa, wb, wc, gf, wr_t, tiled):
    n = oa.shape[0]
    tm = ROW_TILE
    tok = pl.BlockSpec((tm * TOKEN_SUB, LANES), lambda i: (i, 0))
    hspec = tok if tiled else pl.BlockSpec((tm, D_MODEL), lambda i: (i, 0))
    seq_tiles = SEQ // tm
    row = lambda w: pl.BlockSpec((tm, w), lambda i: (i, 0))
    full = lambda a: pl.BlockSpec(a.shape, lambda i: (0,) * a.ndim)
    return pl.pallas_call(
        functools.partial(_out_kernel, tiled=tiled),
        grid=(n // tm,),
        in_specs=[row(A_W), pl.BlockSpec((B_W, tm), lambda i: (0, i)), row(C_W), hspec, full(ga), full(gb), full(gc),
                  full(wa), full(wb), full(wc), full(gf), full(wr_t)],
        out_specs=[tok, tok,
                   pl.BlockSpec((1, N_EXPERTS, tm), lambda i: (i // seq_tiles, 0, i % seq_tiles))],
        out_shape=[jax.ShapeDtypeStruct((n * TOKEN_SUB, LANES), F32), jax.ShapeDtypeStruct((n * TOKEN_SUB, LANES), F32),
                   jax.ShapeDtypeStruct((n // SEQ, N_EXPERTS, SEQ), F32)],
        compiler_params=_cparams(("arbitrary",)),
        name="out_proj_router",
    )(oa, ob, oc, h2, ga, gb, gc, wa, wb, wc, gf, wr_t)


def _topk_kernel(aff_ref, idx_ref, gate_ref, posm_ref):
    rows, seq = aff_ref.shape
    tiles = seq // LANES
    a = aff_ref[...]
    int_min = jnp.int32(-2 ** 31)

    def ordered_to_float(u):
        key = u ^ int_min
        bits = key ^ (lax.shift_right_arithmetic(key, jnp.int32(31)) & jnp.int32(0x7FFFFFFF))
        return lax.bitcast_convert_type(bits, F32)

    t_u = jnp.zeros((rows, 1), jnp.int32)
    for bit in range(31, -1, -1):
        step = int_min if bit == 31 else jnp.int32(1 << bit)
        cand_u = t_u | step
        cnt = jnp.sum(jnp.where(a >= ordered_to_float(cand_u), 1.0, 0.0), axis=1, keepdims=True)
        t_u = jnp.where(cnt >= CAP, cand_u, t_u)
    thr = ordered_to_float(t_u)
    gt = a > thr
    eq = a == thr
    need = CAP - jnp.sum(jnp.where(gt, 1.0, 0.0), axis=1, keepdims=True)

    tri = jnp.where(lax.broadcasted_iota(jnp.int32, (LANES, LANES), 0)
                    <= lax.broadcasted_iota(jnp.int32, (LANES, LANES), 1), 1.0, 0.0).astype(BF16)

    def prefix_incl(flags_f32, t, carry):
        blk = flags_f32[:, t * LANES:(t + 1) * LANES]
        inc = jnp.dot(blk.astype(BF16), tri, preferred_element_type=F32) + carry
        return blk, inc, inc[:, LANES - 1:LANES]

    eq_f = jnp.where(eq, 1.0, 0.0)
    gt_f = jnp.where(gt, 1.0, 0.0)
    carry_eq = jnp.zeros((rows, 1), F32)
    carry_sel = jnp.zeros((rows, 1), F32)
    for t in range(tiles):
        eq_blk, eq_inc, carry_eq = prefix_incl(eq_f, t, carry_eq)
        sel_blk = jnp.maximum(gt_f[:, t * LANES:(t + 1) * LANES],
                              jnp.where(eq_inc <= need, eq_blk, 0.0))
        sel_inc = jnp.dot(sel_blk.astype(BF16), tri, preferred_element_type=F32) + carry_sel
        carry_sel = sel_inc[:, LANES - 1:LANES]
        posm_ref[:, t * LANES:(t + 1) * LANES] = jnp.where(sel_blk > 0.0, sel_inc - 1.0, -1.0)

    slot = lax.broadcasted_iota(jnp.int32, (CAP, LANES), 0).astype(F32)
    sub = lax.broadcasted_iota(jnp.int32, (16, LANES), 0)
    sub8 = lax.broadcasted_iota(jnp.int32, (8, CAP), 0)
    lane =lax.broadcasted_iota(jnp.int32, (1, LANES), 1)

    def tile_body(t, accs, r8):
        c0 = pl.multiple_of(t * LANES, LANES)
        pos8 = posm_ref[pl.ds(r8, 8), pl.ds(c0, LANES)]
        aff8 = aff_ref[pl.ds(r8, 8), pl.ds(c0, LANES)]
        tok = lane + t * LANES
        hi = lax.shift_right_logical(tok, 6).astype(F32)
        lo = (tok & 63).astype(F32)
        out = []
        for k in range(8):
            onehot = jnp.where(pos8[k:k + 1, :] == slot, 1.0, 0.0).astype(BF16)
            av = aff8[k:k + 1, :]
            g0 = av.astype(BF16).astype(F32)
            r1 = av - g0
            g1 = r1.astype(BF16).astype(F32)
            g2 = r1 - g1
            data = jnp.zeros((16, LANES), F32)
            for r, piece in enumerate((hi, lo, g0, g1, g2)):
                data = jnp.where(sub == r, jnp.broadcast_to(piece, (16, LANES)), data)
            out.append(accs[k] + _nt_dot(data.astype(BF16), onehot))
        return tuple(out)

    def row_body(i8, carry):
        r8 = pl.multiple_of(i8 * 8, 8)
        accs = lax.fori_loop(0, tiles, functools.partial(tile_body, r8=r8),
                             tuple(jnp.zeros((16, CAP), F32) for _ in range(8)))
        idx8 = jnp.zeros((8, CAP), F32)
        gate8 = jnp.zeros((8, CAP), F32)
        for k in range(8):
            acc = accs[k]
            idx8 = jnp.where(sub8 == k, jnp.broadcast_to(acc[0:1] * 64.0 + acc[1:2], (8, CAP)), idx8)
            gate8 = jnp.where(sub8 == k, jnp.broadcast_to(acc[2:3] + acc[3:4] + acc[4:5], (8, CAP)), gate8)
        idx_ref[pl.ds(r8, 8), :] = idx8.astype(jnp.int32)
        gate_ref[pl.ds(r8, 8), :] = gate8
        return carry

    lax.fori_loop(0, rows // 8, row_body, 0)


def _topk_call(aff2):
    rows, seq = aff2.shape
    return pl.pallas_call(
        _topk_kernel,
        out_shape=[jax.ShapeDtypeStruct((rows, CAP), jnp.int32), jax.ShapeDtypeStruct((rows, CAP), F32)],
        scratch_shapes=[pltpu.VMEM((rows, seq), F32)],
        compiler_params=pltpu.CompilerParams(vmem_limit_bytes=VMEM_LIMIT),
        name="expert_choice_topk",
    )(aff2)


def _ffn_kernel(idx_ref, x_hbm, wg_ref, wu_ref, wd_ref, y_ref, xs_ref, wbf_ref, sem_ref):
    e = pl.program_id(0)
    b = pl.program_id(1)
    nb = pl.num_programs(1)
    steps = N_EXPERTS * nb
    t = e * nb + b

    def issue(step, slot):
        sb = step % nb
        base = (sb * N_EXPERTS + step // nb) * CAP
        row0 = sb * SEQ

        def body(c, carry):
            tok = idx_ref[base + c]
            src = pl.multiple_of((row0 + tok) * TOKEN_SUB, TOKEN_SUB)
            dst = pl.multiple_of(c * TOKEN_SUB, TOKEN_SUB)
            pltpu.make_async_copy(x_hbm.at[pl.ds(src, TOKEN_SUB), :], xs_ref.at[slot, pl.ds(dst, TOKEN_SUB), :],
                                  sem_ref.at[slot]).start()
            return carry

        lax.fori_loop(0, CAP, body, 0, unroll=8)

    @pl.when(t == 0)
    def _():
        issue(t, 0)

    @pl.when(t + 1 < steps)
    def _():
        issue(t + 1, (t + 1) % 2)

    @pl.when(b == 0)
    def _():
        wbf_ref[0] = wg_ref[0].astype(BF16)
        wbf_ref[1] = wu_ref[0].astype(BF16)
        wbf_ref[2] = wd_ref[0].astype(BF16)

    slot = t % 2
    pltpu.make_async_copy(x_hbm.at[pl.ds(0, CAP * TOKEN_SUB), :], xs_ref.at[slot], sem_ref.at[slot]).wait()
    xs = _load_token_rows(xs_ref.at[slot], CAP).astype(BF16)
    gate = jnp.dot(xs, wbf_ref[0], preferred_element_type=F32)
    up = jnp.dot(xs, wbf_ref[1], preferred_element_type=F32)
    hid = (gate * (1.0 / (1.0 + jnp.exp(-gate))) * up).astype(BF16)
    _store_token_rows(y_ref.at[0, 0], jnp.dot(hid, wbf_ref[2], preferred_element_type=F32))


def _ffn_call(idx_flat, xn3, wg, wu, wd):
    b = xn3.shape[0] // (SEQ * TOKEN_SUB)
    wspec = pl.BlockSpec((1, D_MODEL, D_MODEL), lambda e, bi, idx: (e, 0, 0))
    return pl.pallas_call(
        _ffn_kernel,
        grid_spec=pltpu.PrefetchScalarGridSpec(
            num_scalar_prefetch=1,
            grid=(N_EXPERTS, b),
            in_specs=[pl.BlockSpec(memory_space=pl.ANY), wspec, wspec, wspec],
            out_specs=pl.BlockSpec((1, 1, CAP * TOKEN_SUB, LANES), lambda e, bi, idx: (bi, e, 0, 0)),
            scratch_shapes=[pltpu.VMEM((2, CAP * TOKEN_SUB, LANES), F32),
                            pltpu.VMEM((3, D_MODEL, D_MODEL), BF16),
                            pltpu.SemaphoreType.DMA((2,))],
        ),
        out_shape=jax.ShapeDtypeStruct((b, N_EXPERTS, CAP * TOKEN_SUB, LANES), F32),
        compiler_params=_cparams(("arbitrary", "arbitrary")),
        name="expert_ffn",
    )(idx_flat, xn3, wg, wu, wd)


def _combine_kernel(idx_ref, gate_ref, y_ref, h_hbm, o_hbm, acc_ref, sem_ref):
    b = pl.program_id(0)
    e = pl.program_id(1)

    @pl.when(e == 0)
    def _():
        cp = pltpu.make_async_copy(h_hbm.at[b], acc_ref, sem_ref.at[0])
        cp.start()
        cp.wait()

    base = (b * N_EXPERTS + e) * CAP
    group = 8

    def body(c8, carry):
        new = []
        for k in range(group):
            c = c8 * group + k
            dst = pl.multiple_of(idx_ref[base + c] * TOKEN_SUB, TOKEN_SUB)
            src = pl.multiple_of(c * TOKEN_SUB, TOKEN_SUB)
            new.append((dst, acc_ref[pl.ds(dst, TOKEN_SUB), :]
                        + y_ref[0, 0, pl.ds(src, TOKEN_SUB), :] * gate_ref[base + c]))
        for dst, val in new:
            acc_ref[pl.ds(dst, TOKEN_SUB), :] = val
        return carry

    lax.fori_loop(0, CAP // group, body, 0)

    @pl.when(e == N_EXPERTS - 1)
    def _():
        cp = pltpu.make_async_copy(acc_ref, o_hbm.at[b], sem_ref.at[0])
        cp.start()
        cp.wait()


def _combine_call(idx_flat, gate_flat, y, h3):
    b = h3.shape[0]
    return pl.pallas_call(
        _combine_kernel,
        grid_spec=pltpu.PrefetchScalarGridSpec(
            num_scalar_prefetch=2,
            grid=(b, N_EXPERTS),
            in_specs=[pl.BlockSpec((1, 1, CAP * TOKEN_SUB, LANES), lambda bi, e, idx, gt: (bi, e, 0, 0)),
                      pl.BlockSpec(memory_space=pl.ANY)],
            out_specs=pl.BlockSpec(memory_space=pl.ANY),
            scratch_shapes=[pltpu.VMEM((SEQ * TOKEN_SUB, LANES), F32), pltpu.SemaphoreType.DMA((1,))],
        ),
        out_shape=jax.ShapeDtypeStruct(h3.shape, F32),
        compiler_params=_cparams(("arbitrary", "arbitrary")),
        name="expert_combine",
    )(idx_flat, gate_flat, y, h3)


def _norm_kernel(x_ref, g_ref, o_ref):
    o_ref[...] = _rms(_load_token_rows(x_ref, ROW_TILE), g_ref[...])


def _norm_call(x3, gain):
    n = x3.shape[0] // TOKEN_SUB
    tm = ROW_TILE
    return pl.pallas_call(
        _norm_kernel,
        grid=(n // tm,),
        in_specs=[pl.BlockSpec((tm * TOKEN_SUB, LANES), lambda i: (i, 0)),
                  pl.BlockSpec((1, D_MODEL), lambda i: (0, 0))],
        out_specs=pl.BlockSpec((tm, D_MODEL), lambda i: (i, 0)),
        out_shape=jax.ShapeDtypeStruct((n, D_MODEL), F32),
        compiler_params=_cparams(("arbitrary",)),
        name="final_norm",
    )(x3, gain)


def _rope_table(dim, lead):
    half = dim // 2
    inv = 1.0 / (ROPE_THETA ** (jnp.arange(0, dim, 2, dtype=F32) / dim))
    ang = jnp.arange(SEQ, dtype=F32)[:, None] * inv[None, :]
    cos, sin = jnp.cos(ang), jnp.sin(ang)
    zero = jnp.zeros_like(sin)
    if lead:
        tail = jnp.zeros((SEQ, LANES - lead - dim), F32)
        ones = jnp.ones((SEQ, lead), F32)
        zl = jnp.zeros((SEQ, lead), F32)
        c = jnp.concatenate([ones, cos, cos, tail], axis=1)
        sa = jnp.concatenate([zl, zero, sin, tail], axis=1)
        sb = jnp.concatenate([zl, -sin, zero, tail], axis=1)
    else:
        reps = LANES // dim
        c = jnp.concatenate([cos, cos] * reps, axis=1)
        sa = jnp.concatenate([zero, sin] * reps, axis=1)
        sb = jnp.concatenate([-sin, zero] * reps, axis=1)
    return jnp.concatenate([c, sa, sb], axis=1)


def _swa_head_perm():
    return [kvh * SWA_GROUP + g for g in range(SWA_GROUP) for kvh in range(SWA_KV_HEADS)]


def _permute_w_in(w):
    offs = np.cumsum([0, A_W, A_W, A_W, MLA_Q_LORA, MLA_KV_LORA, MLA_ROPE, C_W, 2 * HEAD_DIM, 2 * HEAD_DIM])
    a_q, a_k, a_v, b_cq, b_ckv, b_kr, c_q, c_k, c_v = [w[:, offs[i]:offs[i + 1]] for i in range(9)]
    c_q = jnp.concatenate([c_q[:, h * HEAD_DIM:(h + 1) * HEAD_DIM] for h in _swa_head_perm()], axis=1)
    zeros = lambda n: jnp.zeros((w.shape[0], n), w.dtype)
    kr = jnp.concatenate([zeros(MLA_NOPE), b_kr, zeros(LANES - MLA_NOPE - MLA_ROPE)], axis=1)
    return jnp.concatenate([a_q, a_k, a_v, b_cq, b_ckv, c_q, c_k, c_v, kr], axis=1).astype(BF16)


def _permute_mla(w_uq, w_ukv):
    zq = jnp.zeros((MLA_Q_LORA, LANES - MLA_NOPE - MLA_ROPE), w_uq.dtype)
    zk = jnp.zeros((MLA_KV_LORA, LANES - MLA_NOPE), w_ukv.dtype)
    dq = MLA_NOPE + MLA_ROPE
    dkv = MLA_NOPE + MLA_V
    wq = jnp.concatenate([jnp.concatenate([w_uq[:, h * dq:(h + 1) * dq], zq], axis=1)
                          for h in range(MLA_HEADS)], axis=1)
    wk = jnp.concatenate([jnp.concatenate([w_ukv[:, h * dkv:h * dkv + MLA_NOPE], zk], axis=1)
                          for h in range(MLA_HEADS)], axis=1)
    wv = jnp.concatenate([w_ukv[:, h * dkv + MLA_NOPE:(h + 1) * dkv] for h in range(MLA_HEADS)], axis=1)
    return wq.astype(BF16), jnp.concatenate([wk, wv], axis=1).astype(BF16)


def kernel(x, attn_norm, w_in, na_rpb, mla_q_norm, mla_w_uq, mla_kv_norm, mla_w_ukv, swa_sink, group_norm,
           w_out, ffn_norm, w_router, w_gate, w_up, w_down, final_norm):
    bsz, seq, d = x.shape
    assert (seq, d) == (SEQ, D_MODEL)
    n = bsz * seq
    depth = w_in.shape[0]
    rope_s = _rope_table(HEAD_DIM, 0)
    rope_m = _rope_table(MLA_ROPE, MLA_NOPE)
    c_perm = np.concatenate([np.arange(h * HEAD_DIM, (h + 1) * HEAD_DIM) for h in _swa_head_perm()])

    h2 = x.reshape(n, d)
    for l in range(depth):
        tiled = l > 0
        wq, wkv = _permute_mla(mla_w_uq[l], mla_w_ukv[l])
        aq, ak, av, mq, mk, mv, sq, sk, sv = _proj_call(
            h2, attn_norm[l][None], _permute_w_in(w_in[l]), mla_q_norm[l][None], wq, mla_kv_norm[l][None], wkv,
            rope_s, rope_m, tiled)
        r3 = lambda a: a.reshape(bsz, seq, a.shape[-1])
        o_a = _na_call(r3(aq), r3(ak), r3(av), _na_bias_tables(na_rpb[l]))
        o_bt = _mla_call(r3(mq), r3(mk), mv)
        o_c = _swa_call(swa_sink[l], r3(sq), r3(sk), r3(sv))

        gn = group_norm[l]
        wo = w_out[l]
        gc = gn[A_W + B_W:][c_perm]
        wc = wo[A_W + B_W:][c_perm]
        hm, xn3, aff_t = _out_call(
            o_a.reshape(n, A_W), o_bt, o_c.reshape(n, C_W), h2,
            gn[None, :A_W], gn[A_W:A_W + B_W, None], gc[None],
            wo[:A_W].astype(BF16), wo[A_W:A_W + B_W].astype(BF16), wc.astype(BF16),
            ffn_norm[l][None], w_router[l].T, tiled)

        idx, gate = _topk_call(aff_t.reshape(bsz * N_EXPERTS, seq))
        idx_flat = idx.reshape(-1)
        y = _ffn_call(idx_flat, xn3, w_gate[l], w_up[l], w_down[l])
        h2 = _combine_call(idx_flat, gate.reshape(-1), y,
                           hm.reshape(bsz, seq * TOKEN_SUB, LANES)).reshape(n * TOKEN_SUB, LANES)
    return _norm_call(h2, final_norm[None]).reshape(bsz, seq, d)
```

```python
import functools

import jax
import jax.numpy as jnp
import numpy as np
from jax import lax
from jax.experimental import pallas as pl
from jax.experimental.pallas import tpu as pltpu

F32 = jnp.float32
BF16 = jnp.bfloat16

D_MODEL = 1024
SEQ = 4096
HEAD_DIM = 64
GRID_W = 64
NA_HEADS = 4
NA_WIN_H = 8
NA_WIN_W = 16
MLA_HEADS = 6
MLA_Q_LORA = 256
MLA_KV_LORA = 128
MLA_NOPE = 64
MLA_ROPE = 32
MLA_V = 64
SWA_HEADS = 6
SWA_KV_HEADS = 2
SWA_GROUP = SWA_HEADS // SWA_KV_HEADS
SWA_WINDOW = 128
ROPE_THETA = 10000.0
N_EXPERTS = 16
EC_CAPACITY = 2
CAP = EC_CAPACITY * SEQ // N_EXPERTS
RMS_EPS = 1e-6
NEG_INF = -1e30

A_W = NA_HEADS * HEAD_DIM
B_W = MLA_HEADS * MLA_V
C_W = SWA_HEADS * HEAD_DIM

LANES = 128
TOKEN_SUB = D_MODEL // LANES
ROW_TILE = 512
VMEM_LIMIT = 56 * 1024 * 1024

P_AQ, P_AK, P_AV = 0, 256, 512
P_CQ, P_CKV = 768, 1024
P_SQ, P_SK, P_SV = 1152, 1536, 1664
P_KR = 1792
P_COLS = 1920

NA_QROWS = 4
NA_KROWS = 12
NA_TQ = NA_QROWS * GRID_W
NA_TK = NA_KROWS * GRID_W
MLA_TQ = 512
MLA_TK = 256
MLA_VT_ROWS = MLA_V + 16
LOG2_E = 1.4426950408889634
FFN_CHUNKS = 4
SWA_TQ = 512
SWA_SUB = 256
SWA_TK = SWA_SUB + 2 * SWA_WINDOW


def _cparams(sem):
    return pltpu.CompilerParams(dimension_semantics=sem, vmem_limit_bytes=VMEM_LIMIT)


def _rms(x, gain):
    ms = jnp.mean(x * x, axis=-1, keepdims=True)
    return x * lax.rsqrt(ms + RMS_EPS) * gain


def _load_token_rows(ref, rows):
    return jnp.concatenate([ref[pl.ds(s, rows, stride=TOKEN_SUB), :] for s in range(TOKEN_SUB)], axis=1)


def _store_token_rows(ref, val):
    rows = val.shape[0]
    for s in range(TOKEN_SUB):
        ref[pl.ds(s, rows, stride=TOKEN_SUB), :] = val[:, s * LANES:(s + 1) * LANES]


def _nt_dot(a, b):
    return lax.dot_general(a, b, (((1,), (1,)), ((), ())), preferred_element_type=F32)


def _proj_kernel(x_ref, gain_ref, win_ref, qn_ref, wq_ref, kvn_ref, wkv_ref, rs_ref, rm_ref,
                 aq_ref, ak_ref, av_ref, mq_ref, mk_ref, mv_ref, sq_ref, sk_ref, sv_ref, *, tiled):
    x = _load_token_rows(x_ref, ROW_TILE) if tiled else x_ref[...]
    xn = _rms(x, gain_ref[...]).astype(BF16)
    proj = jnp.dot(xn, win_ref[...], preferred_element_type=F32)
    lane = lax.broadcasted_iota(jnp.int32, (1, LANES), 1)
    lo_half = lane < HEAD_DIM

    def rope(grp, tab_ref, half):
        c = tab_ref[:, 0:LANES]
        sa = tab_ref[:, LANES:2 * LANES]
        sb = tab_ref[:, 2 * LANES:3 * LANES]
        return grp * c + pltpu.roll(grp, half, 1) * sa + pltpu.roll(grp, LANES - half, 1) * sb

    def split_heads(grp):
        zero = jnp.zeros_like(grp)
        return jnp.where(lo_half, grp, zero).astype(BF16), jnp.where(lo_half, zero, grp).astype(BF16)

    na_scale = HEAD_DIM ** -0.5 * LOG2_E
    for p in range(NA_HEADS // 2):
        grp = proj[:, P_AQ + p * LANES:P_AQ + (p + 1) * LANES] * na_scale
        q0, q1 = split_heads(grp)
        aq_ref[:, (2 * p) * LANES:(2 * p + 1) * LANES] = q0
        aq_ref[:, (2 * p + 1) * LANES:(2 * p + 2) * LANES] = q1
    ak_ref[...] = proj[:, P_AK:P_AK + A_W].astype(BF16)
    av_ref[...] = proj[:, P_AV:P_AV + A_W].astype(BF16)

    cq = _rms(proj[:, P_CQ:P_CQ + MLA_Q_LORA], qn_ref[...]).astype(BF16)
    q = jnp.dot(cq, wq_ref[...], preferred_element_type=F32) * ((MLA_NOPE + MLA_ROPE) ** -0.5 * LOG2_E)
    ckv = _rms(proj[:, P_CKV:P_CKV + MLA_KV_LORA], kvn_ref[...]).astype(BF16)
    kv = jnp.dot(ckv, wkv_ref[...], preferred_element_type=F32)
    kr = rope(proj[:, P_KR:P_KR + LANES], rm_ref, MLA_ROPE // 2)
    for h in range(MLA_HEADS):
        sl = slice(h * LANES, (h + 1) * LANES)
        mq_ref[:, sl] = rope(q[:, sl], rm_ref, MLA_ROPE // 2).astype(BF16)
        mk_ref[:, sl] = (kv[:, sl] + kr).astype(BF16)
    vt = kv[:, MLA_HEADS * LANES:MLA_HEADS * LANES + B_W].T
    pad_rows = MLA_VT_ROWS - MLA_V
    ones_row = jnp.where(lax.broadcasted_iota(jnp.int32, (pad_rows, vt.shape[1]), 0) == 0, 1.0, 0.0)
    for h in range(MLA_HEADS):
        blk = jnp.concatenate([vt[h * MLA_V:(h + 1) * MLA_V], ones_row], axis=0)
        mv_ref[h * MLA_VT_ROWS:(h + 1) * MLA_VT_ROWS, :] = blk.astype(BF16)

    swa_scale = HEAD_DIM ** -0.5 * LOG2_E
    for g in range(SWA_GROUP):
        grp = rope(proj[:, P_SQ + g * LANES:P_SQ + (g + 1) * LANES], rs_ref, HEAD_DIM // 2) * swa_scale
        q0, q1 = split_heads(grp)
        sq_ref[:, (2 * g) * LANES:(2 * g + 1) * LANES] = q0
        sq_ref[:, (2 * g + 1) * LANES:(2 * g + 2) * LANES] = q1
    sk_ref[...] = rope(proj[:, P_SK:P_SK + LANES], rs_ref, HEAD_DIM // 2).astype(BF16)
    sv_ref[...] = proj[:, P_SV:P_SV + LANES].astype(BF16)


def _proj_call(x2, gain, win, qn, wq, kvn, wkv, rope_s, rope_m, tiled):
    tm = ROW_TILE
    n = x2.shape[0] // TOKEN_SUB if tiled else x2.shape[0]
    xspec = pl.BlockSpec((tm * TOKEN_SUB, LANES) if tiled else (tm, D_MODEL), lambda i: (i, 0))
    seq_tiles = SEQ // tm
    row = lambda w: pl.BlockSpec((tm, w), lambda i: (i, 0))
    full = lambda a: pl.BlockSpec(a.shape, lambda i: (0,) * a.ndim)
    pos = lambda w: pl.BlockSpec((tm, w), lambda i: (i % seq_tiles, 0))
    widths = (2 * A_W, A_W, A_W, MLA_HEADS * LANES, MLA_HEADS * LANES, B_W, 2 * C_W, LANES, LANES)
    mv_pos = widths.index(B_W)
    out_specs = [row(w) for w in widths]
    out_shape = [jax.ShapeDtypeStruct((n, w), BF16) for w in widths]
    out_specs[mv_pos] = pl.BlockSpec((MLA_HEADS * MLA_VT_ROWS, tm), lambda i: (0, i))
    out_shape[mv_pos] = jax.ShapeDtypeStruct((MLA_HEADS * MLA_VT_ROWS, n), BF16)
    return pl.pallas_call(
        functools.partial(_proj_kernel, tiled=tiled),
        grid=(n // tm,),
        in_specs=[xspec, full(gain), full(win), full(qn), full(wq), full(kvn), full(wkv),
                  pos(3 * LANES), pos(3 * LANES)],
        out_specs=out_specs,
        out_shape=out_shape,
        compiler_params=_cparams(("arbitrary",)),
        name="norm_in_proj",
    )(x2, gain, win, qn, wq, kvn, wkv, rope_s, rope_m)


def _na_key_start(j):
    rows = SEQ // GRID_W
    return jnp.clip(NA_QROWS * j - NA_WIN_H // 2, 0, rows - NA_KROWS)


def _na_kernel(q_ref, k_ref, v_ref, bias_ref, o_ref):
    j = pl.program_id(1)
    ks = pl.multiple_of(_na_key_start(j) * GRID_W, 256)
    kwin = k_ref[0, pl.ds(ks, NA_TK), :]
    vwin = v_ref[0, pl.ds(ks, NA_TK), :]
    lo_half = lax.broadcasted_iota(jnp.int32, (1, LANES), 1) < HEAD_DIM
    for p in range(NA_HEADS // 2):
        kp = kwin[:, p * LANES:(p + 1) * LANES]
        vp = vwin[:, p * LANES:(p + 1) * LANES]
        outs = []
        for hh in range(2):
            h = 2 * p + hh
            s = _nt_dot(q_ref[0, :, h * LANES:(h + 1) * LANES], kp) + bias_ref[0, h]
            m = jnp.max(s, axis=-1, keepdims=True)
            e = jnp.exp2(s - m)
            l = jnp.sum(e, axis=-1, keepdims=True)
            outs.append(jnp.dot(e.astype(BF16), vp, preferred_element_type=F32) / l)
        o_ref[0, :, p * LANES:(p + 1) * LANES] = jnp.where(lo_half, outs[0], outs[1])


def _na_call(aq, ak, av, bias):
    b = aq.shape[0]
    nj = SEQ // NA_TQ

    def variant(bi, j):
        return (jnp.where(j == 0, 0, jnp.where(j == nj - 1, 2, 1)), 0, 0, 0)

    return pl.pallas_call(
        _na_kernel,
        grid=(b, nj),
        in_specs=[pl.BlockSpec((1, NA_TQ, 2 * A_W), lambda bi, j: (bi, j, 0)),
                  pl.BlockSpec((1, SEQ, A_W), lambda bi, j: (bi, 0, 0)),
                  pl.BlockSpec((1, SEQ, A_W), lambda bi, j: (bi, 0, 0)),
                  pl.BlockSpec((1, NA_HEADS, NA_TQ, NA_TK), variant)],
        out_specs=pl.BlockSpec((1, NA_TQ, A_W), lambda bi, j: (bi, j, 0)),
        out_shape=jax.ShapeDtypeStruct((b, SEQ, A_W), F32),
        compiler_params=_cparams(("arbitrary", "arbitrary")),
        name="neighbourhood_attn",
    )(aq, ak, av, bias)


def _na_bias_tables(rpb):
    rows = SEQ // GRID_W
    nj = SEQ // NA_TQ
    c = np.arange(GRID_W)[:, None]
    kc = np.arange(GRID_W)[None, :]
    cs = np.clip(c - NA_WIN_W // 2, 0, GRID_W - NA_WIN_W)
    cvalid = (kc >= cs) & (kc < cs + NA_WIN_W)
    pad = GRID_W - NA_WIN_W
    padded = jnp.pad(rpb, ((0, 0), (0, 0), (pad, pad)))
    toep = jnp.stack([padded[:, :, GRID_W - 1 - ci:2 * GRID_W - 1 - ci] for ci in range(GRID_W)], axis=2)
    toep = jnp.where(cvalid[None, None], toep * LOG2_E, NEG_INF)
    masked = jnp.full((rpb.shape[0], GRID_W, GRID_W), NEG_INF, F32)
    variants = []
    for j in (0, 1, nj - 1):
        ks = int(np.clip(NA_QROWS * j - NA_WIN_H // 2, 0, rows - NA_KROWS))
        row_blocks = []
        for i in range(NA_QROWS):
            r = NA_QROWS * j + i
            rs = int(np.clip(r - NA_WIN_H // 2, 0, rows - NA_WIN_H))
            blocks = []
            for u in range(NA_KROWS):
                kr = ks + u
                blocks.append(toep[:, kr - r + NA_WIN_H - 1] if rs <= kr < rs + NA_WIN_H else masked)
            row_blocks.append(jnp.concatenate(blocks, axis=2))
        variants.append(jnp.concatenate(row_blocks, axis=1))
    return jnp.stack(variants)


def _mla_kernel(q_ref, k_ref, vt_ref, o_ref):
    heads = range(2)
    nchunks = SEQ // MLA_TK

    def scores(c, hh):
        sl = slice(hh * LANES, (hh + 1) * LANES)
        return _nt_dot(k_ref[0, c * MLA_TK:(c + 1) * MLA_TK, sl], q_ref[0, :, sl])

    m = [jnp.full((1, MLA_TQ), NEG_INF, F32) for _ in heads]
    acc = [jnp.zeros((MLA_VT_ROWS, MLA_TQ), F32) for _ in heads]
    st = [scores(0, hh) for hh in heads]
    for c in range(nchunks):
        st_next = [scores(c + 1, hh) for hh in heads] if c + 1 < nchunks else None
        for hh in heads:
            m_new = jnp.maximum(m[hh], jnp.max(st[hh], axis=0, keepdims=True))
            alpha = jnp.exp2(m[hh] - m_new)
            e = jnp.exp2(st[hh] - m_new).astype(BF16)
            vt = vt_ref[hh * MLA_VT_ROWS:(hh + 1) * MLA_VT_ROWS, c * MLA_TK:(c + 1) * MLA_TK]
            acc[hh] = alpha * acc[hh] + jnp.dot(vt, e, preferred_element_type=F32)
            m[hh] = m_new
        st = st_next
    o_ref[...] = jnp.concatenate([acc[hh][:MLA_V] / acc[hh][MLA_V:MLA_V + 1] for hh in heads], axis=0)


def _mla_call(mq, mk, mvt):
    b = mq.shape[0]
    pairs = MLA_HEADS // 2
    nq = SEQ // MLA_TQ
    return pl.pallas_call(
        _mla_kernel,
        grid=(b, pairs, nq),
        in_specs=[pl.BlockSpec((1, MLA_TQ, 2 * LANES), lambda bi, p, i: (bi, i, p)),
                  pl.BlockSpec((1, SEQ, 2 * LANES), lambda bi, p, i: (bi, 0, p)),
                  pl.BlockSpec((2 * MLA_VT_ROWS, SEQ), lambda bi, p, i: (p, bi))],
        out_specs=pl.BlockSpec((2 * MLA_V, MLA_TQ), lambda bi, p, i: (p, bi * nq + i)),
        out_shape=jax.ShapeDtypeStruct((B_W, b * SEQ), F32),
        compiler_params=_cparams(("arbitrary", "arbitrary", "arbitrary")),
        name="latent_attn",
    )(mq, mk, mvt)


def _swa_kernel(sink_ref, q_ref, k_ref, v_ref, o_ref):
    i = pl.program_id(1)
    lo_half = lax.broadcasted_iota(jnp.int32, (1, LANES), 1) < HEAD_DIM
    for sb in range(SWA_TQ // SWA_SUB):
        rows = slice(sb * SWA_SUB, (sb + 1) * SWA_SUB)
        q0 = SWA_TQ * i + sb * SWA_SUB
        ws = pl.multiple_of(jnp.clip(q0 - SWA_WINDOW, 0, SEQ - SWA_TK), LANES)
        kwin = k_ref[0, pl.ds(ws, SWA_TK), :]
        vwin = v_ref[0, pl.ds(ws, SWA_TK), :]
        qpos = q0 + lax.broadcasted_iota(jnp.int32, (SWA_SUB, 1), 0)
        kpos = ws + lax.broadcasted_iota(jnp.int32, (1, SWA_TK), 1)
        valid = jnp.abs(qpos - kpos) <= SWA_WINDOW
        for g in range(SWA_GROUP):
            outs = []
            for kvh in range(SWA_KV_HEADS):
                grp = 2 * g + kvh
                s = _nt_dot(q_ref[0, rows, grp * LANES:(grp + 1) * LANES], kwin)
                s = jnp.where(valid, s, NEG_INF)
                sink = sink_ref[kvh * SWA_GROUP + g] * LOG2_E
                m = jnp.maximum(jnp.max(s, axis=-1, keepdims=True), sink)
                e = jnp.exp2(s - m)
                l = jnp.sum(e, axis=-1, keepdims=True) + jnp.exp2(sink - m)
                outs.append(jnp.dot(e.astype(BF16), vwin, preferred_element_type=F32) / l)
            o_ref[0, rows, g * LANES:(g + 1) * LANES] = jnp.where(lo_half, outs[0], outs[1])


def _swa_call(sink, sq, sk, sv):
    b = sq.shape[0]
    return pl.pallas_call(
        _swa_kernel,
        grid=(b, SEQ // SWA_TQ),
        in_specs=[pl.BlockSpec(memory_space=pltpu.SMEM),
                  pl.BlockSpec((1, SWA_TQ, 2 * C_W), lambda bi, i: (bi, i, 0)),
                  pl.BlockSpec((1, SEQ, LANES), lambda bi, i: (bi, 0, 0)),
                  pl.BlockSpec((1, SEQ, LANES), lambda bi, i: (bi, 0, 0))],
        out_specs=pl.BlockSpec((1, SWA_TQ, C_W), lambda bi, i: (bi, i, 0)),
        out_shape=jax.ShapeDtypeStruct((b, SEQ, C_W), F32),
        compiler_params=_cparams(("arbitrary", "arbitrary")),
        name="window_gqa",
    )(sink, sq, sk, sv)


def _out_kernel(oa_ref, ob_ref, oc_ref, h_ref, ga_ref, gb_ref, gc_ref, wa_ref, wb_ref, wc_ref,
                gf_ref, wr_ref, hm_ref, xn_ref, aff_ref, *, tiled):
    acc = _load_token_rows(h_ref, ROW_TILE) if tiled else h_ref[...]
    for o_ref, g_ref, w_ref in ((oa_ref, ga_ref, wa_ref), (oc_ref, gc_ref, wc_ref)):
        acc = acc + jnp.dot(_rms(o_ref[...], g_ref[...]).astype(BF16), w_ref[...], preferred_element_type=F32)
    ob = ob_ref[...]
    ms = jnp.mean(ob * ob, axis=0, keepdims=True)
    obn = (ob * lax.rsqrt(ms + RMS_EPS) * gb_ref[...]).astype(BF16)
    acc = acc + lax.dot_general(obn, wb_ref[...], (((0,), (0,)), ((), ())), preferred_element_type=F32)
    _store_token_rows(hm_ref, acc)
    xn = _rms(acc, gf_ref[...])
    _store_token_rows(xn_ref, xn)
    logits = lax.dot_general(wr_ref[...], xn, (((1,), (1,)), ((), ())),
                             precision=lax.Precision.HIGHEST, preferred_element_type=F32)
    m = jnp.max(logits, axis=0, keepdims=True)
    e = jnp.exp(logits - m)
    aff_ref[0] = e / jnp.sum(e, axis=0, keepdims=True)


def _out_call(oa, ob, oc, h2, ga, gb, gc, wa, wb, wc, gf, wr_t, tiled):
    n = oa.shape[0]
    tm = ROW_TILE
    tok = pl.BlockSpec((tm * TOKEN_SUB, LANES), lambda i: (i, 0))
    hspec = tok if tiled else pl.BlockSpec((tm, D_MODEL), lambda i: (i, 0))
    seq_tiles = SEQ // tm
    row = lambda w: pl.BlockSpec((tm, w), lambda i: (i, 0))
    full = lambda a: pl.BlockSpec(a.shape, lambda i: (0,) * a.ndim)
    return pl.pallas_call(
        functools.partial(_out_kernel, tiled=tiled),
        grid=(n // tm,),
        in_specs=[row(A_W), pl.BlockSpec((B_W, tm), lambda i: (0, i)), row(C_W), hspec, full(ga), full(gb), full(gc),
                  full(wa), full(wb), full(wc), full(gf), full(wr_t)],
        out_specs=[tok, tok,
                   pl.BlockSpec((1, N_EXPERTS, tm), lambda i: (i // seq_tiles, 0, i % seq_tiles))],
        out_shape=[jax.ShapeDtypeStruct((n * TOKEN_SUB, LANES), F32), jax.ShapeDtypeStruct((n * TOKEN_SUB, LANES), F32),
                   jax.ShapeDtypeStruct((n // SEQ, N_EXPERTS, SEQ), F32)],
        compiler_params=_cparams(("arbitrary",)),
        name="out_proj_router",
    )(oa, ob, oc, h2, ga, gb, gc, wa, wb, wc, gf, wr_t)


def _topk_kernel(aff_ref, idx_ref, gate_ref, posm_ref):
    rows, seq = aff_ref.shape
    tiles = seq // LANES
    a = aff_ref[...]
    int_min = jnp.int32(-2 ** 31)

    def ordered_to_float(u):
        key = u ^ int_min
        bits = key ^ (lax.shift_right_arithmetic(key, jnp.int32(31)) & jnp.int32(0x7FFFFFFF))
        return lax.bitcast_convert_type(bits, F32)

    t_u = jnp.zeros((rows, 1), jnp.int32)
    for bit in range(31, -1, -1):
        step = int_min if bit == 31 else jnp.int32(1 << bit)
        cand_u = t_u | step
        cnt = jnp.sum(jnp.where(a >= ordered_to_float(cand_u), 1.0, 0.0), axis=1, keepdims=True)
        t_u = jnp.where(cnt >= CAP, cand_u, t_u)
    thr = ordered_to_float(t_u)
    gt = a > thr
    eq = a == thr
    need = CAP - jnp.sum(jnp.where(gt, 1.0, 0.0), axis=1, keepdims=True)

    tri = jnp.where(lax.broadcasted_iota(jnp.int32, (LANES, LANES), 0)
                    <= lax.broadcasted_iota(jnp.int32, (LANES, LANES), 1), 1.0, 0.0).astype(BF16)

    def prefix_incl(flags_f32, t, carry):
        blk = flags_f32[:, t * LANES:(t + 1) * LANES]
        inc = jnp.dot(blk.astype(BF16), tri, preferred_element_type=F32) + carry
        return blk, inc, inc[:, LANES - 1:LANES]

    eq_f = jnp.where(eq, 1.0, 0.0)
    gt_f = jnp.where(gt, 1.0, 0.0)
    carry_eq = jnp.zeros((rows, 1), F32)
    carry_sel = jnp.zeros((rows, 1), F32)
    for t in range(tiles):
        eq_blk, eq_inc, carry_eq = prefix_incl(eq_f, t, carry_eq)
        sel_blk = jnp.maximum(gt_f[:, t * LANES:(t + 1) * LANES],
                              jnp.where(eq_inc <= need, eq_blk, 0.0))
        sel_inc = jnp.dot(sel_blk.astype(BF16), tri, preferred_element_type=F32) + carry_sel
        carry_sel = sel_inc[:, LANES - 1:LANES]
        posm_ref[:, t * LANES:(t + 1) * LANES] = jnp.where(sel_blk > 0.0, sel_inc - 1.0, -1.0)

    slot = lax.broadcasted_iota(jnp.int32, (CAP, LANES), 0).astype(F32)
    sub = lax.broadcasted_iota(jnp.int32, (16, LANES), 0)
    sub8 = lax.broadcasted_iota(jnp.int32, (8, CAP), 0)
    lane =lax.broadcasted_iota(jnp.int32, (1, LANES), 1)

    def tile_body(t, accs, r8):
        c0 = pl.multiple_of(t * LANES, LANES)
        pos8 = posm_ref[pl.ds(r8, 8), pl.ds(c0, LANES)]
        aff8 = aff_ref[pl.ds(r8, 8), pl.ds(c0, LANES)]
        tok = lane + t * LANES
        hi = lax.shift_right_logical(tok, 6).astype(F32)
        lo = (tok & 63).astype(F32)
        out = []
        for k in range(8):
            onehot = jnp.where(pos8[k:k + 1, :] == slot, 1.0, 0.0).astype(BF16)
            av = aff8[k:k + 1, :]
            g0 = av.astype(BF16).astype(F32)
            r1 = av - g0
            g1 = r1.astype(BF16).astype(F32)
            g2 = r1 - g1
            data = jnp.zeros((16, LANES), F32)
            for r, piece in enumerate((hi, lo, g0, g1, g2)):
                data = jnp.where(sub == r, jnp.broadcast_to(piece, (16, LANES)), data)
            out.append(accs[k] + _nt_dot(data.astype(BF16), onehot))
        return tuple(out)

    def row_body(i8, carry):
        r8 = pl.multiple_of(i8 * 8, 8)
        accs = lax.fori_loop(0, tiles, functools.partial(tile_body, r8=r8),
                             tuple(jnp.zeros((16, CAP), F32) for _ in range(8)))
        idx8 = jnp.zeros((8, CAP), F32)
        gate8 = jnp.zeros((8, CAP), F32)
        for k in range(8):
            acc = accs[k]
            idx8 = jnp.where(sub8 == k, jnp.broadcast_to(acc[0:1] * 64.0 + acc[1:2], (8, CAP)), idx8)
            gate8 = jnp.where(sub8 == k, jnp.broadcast_to(acc[2:3] + acc[3:4] + acc[4:5], (8, CAP)), gate8)
        idx_ref[pl.ds(r8, 8), :] = idx8.astype(jnp.int32)
        gate_ref[pl.ds(r8, 8), :] = gate8
        return carry

    lax.fori_loop(0, rows // 8, row_body, 0)


def _topk_call(aff2):
    rows, seq = aff2.shape
    return pl.pallas_call(
        _topk_kernel,
        out_shape=[jax.ShapeDtypeStruct((rows, CAP), jnp.int32), jax.ShapeDtypeStruct((rows, CAP), F32)],
        scratch_shapes=[pltpu.VMEM((rows, seq), F32)],
        compiler_params=pltpu.CompilerParams(vmem_limit_bytes=VMEM_LIMIT),
        name="expert_choice_topk",
    )(aff2)


def _ffn_kernel(idx_ref, x_hbm, wg_ref, wu_ref, wd_ref, y_ref, xs_ref, wbf_ref, sem_ref):
    e = pl.program_id(0)
    b = pl.program_id(1)
    nb = pl.num_programs(1)
    steps = N_EXPERTS * nb
    t = e * nb + b

    def issue(step, slot):
        sb = step % nb
        base = (sb * N_EXPERTS + step // nb) * CAP
        row0 = sb * SEQ

        def body(c, carry):
            tok = idx_ref[base + c]
            src = pl.multiple_of((row0 + tok) * TOKEN_SUB, TOKEN_SUB)
            dst = pl.multiple_of(c * TOKEN_SUB, TOKEN_SUB)
            pltpu.make_async_copy(x_hbm.at[pl.ds(src, TOKEN_SUB), :], xs_ref.at[slot, pl.ds(dst, TOKEN_SUB), :],
                                  sem_ref.at[slot]).start()
            return carry

        lax.fori_loop(0, CAP, body, 0, unroll=8)

    @pl.when(t == 0)
    def _():
        issue(t, 0)

    @pl.when(b == 0)
    def _():
        wbf_ref[0] = wg_ref[0, 0].astype(BF16)
        wbf_ref[1] = wu_ref[0, 0].astype(BF16)
        wbf_ref[2] = wd_ref[0, 0].astype(BF16)

    slot = t % 2
    wait_all = lambda s: pltpu.make_async_copy(x_hbm.at[pl.ds(0, CAP * TOKEN_SUB), :], xs_ref.at[s],
                                               sem_ref.at[s]).wait()
    wait_all(slot)
    xs = _load_token_rows(xs_ref.at[slot], CAP).astype(BF16)

    nstep = jnp.where(t + 1 < steps, t + 1, 0)
    nslot = 1 - slot
    nsb = nstep % nb
    nbase = (nsb * N_EXPERTS + nstep // nb) * CAP
    nrow0 = nsb * SEQ
    pieces = 3 * FFN_CHUNKS
    bounds = [CAP * k // pieces for k in range(pieces + 1)]
    piece = iter(range(pieces))

    def issue_piece():
        k = next(piece)
        for c in range(bounds[k], bounds[k + 1]):
            src = pl.multiple_of((nrow0 + idx_ref[nbase + c]) * TOKEN_SUB, TOKEN_SUB)
            pltpu.make_async_copy(x_hbm.at[pl.ds(src, TOKEN_SUB), :],
                                  xs_ref.at[nslot, pl.ds(c * TOKEN_SUB, TOKEN_SUB), :], sem_ref.at[nslot]).start()

    width = D_MODEL // FFN_CHUNKS
    y = None
    for j in range(FFN_CHUNKS):
        cols = slice(j * width, (j + 1) * width)
        gate = jnp.dot(xs, wbf_ref[0, :, cols], preferred_element_type=F32)
        issue_piece()
        up = jnp.dot(xs, wbf_ref[1, :, cols], preferred_element_type=F32)
        issue_piece()
        hid = (gate * (1.0 / (1.0 + jnp.exp(-gate))) * up).astype(BF16)
        part = jnp.dot(hid, wbf_ref[2, cols, :], preferred_element_type=F32)
        y = part if y is None else y + part
        issue_piece()
    _store_token_rows(y_ref.at[0, 0], y)

    @pl.when(t == steps - 1)
    def _():
        wait_all(nslot)


def _ffn_call(idx_flat, xn3, wg, wu, wd, layer):
    b = xn3.shape[0] // (SEQ * TOKEN_SUB)
    wspec = pl.BlockSpec((1, 1, D_MODEL, D_MODEL), lambda e, bi, idx: (layer, e, 0, 0))
    return pl.pallas_call(
        _ffn_kernel,
        grid_spec=pltpu.PrefetchScalarGridSpec(
            num_scalar_prefetch=1,
            grid=(N_EXPERTS, b),
            in_specs=[pl.BlockSpec(memory_space=pl.ANY), wspec, wspec, wspec],
            out_specs=pl.BlockSpec((1, 1, CAP * TOKEN_SUB, LANES), lambda e, bi, idx: (bi, e, 0, 0)),
            scratch_shapes=[pltpu.VMEM((2, CAP * TOKEN_SUB, LANES), F32),
                            pltpu.VMEM((3, D_MODEL, D_MODEL), BF16),
                            pltpu.SemaphoreType.DMA((2,))],
        ),
        out_shape=jax.ShapeDtypeStruct((b, N_EXPERTS, CAP * TOKEN_SUB, LANES), F32),
        compiler_params=_cparams(("arbitrary", "arbitrary")),
        name="expert_ffn",
    )(idx_flat, xn3, wg, wu, wd)


def _combine_kernel(idx_ref, gate_ref, y_ref, h_hbm, o_hbm, acc_ref, sem_ref):
    b = pl.program_id(0)
    e = pl.program_id(1)
    nb = pl.num_programs(0)
    slot = b % 2
    other = 1 - slot
    load = lambda bb, s: pltpu.make_async_copy(h_hbm.at[bb], acc_ref.at[s], sem_ref.at[0, s])
    drain = lambda bb, s: pltpu.make_async_copy(acc_ref.at[s], o_hbm.at[bb], sem_ref.at[1, s])

    @pl.when((b == 0) & (e == 0))
    def _():
        load(b, slot).start()

    @pl.when(e == 0)
    def _():
        load(b, slot).wait()

    @pl.when((e == 1) & (b >= 1))
    def _():
        drain(b - 1, other).wait()

    @pl.when((e == 1) & (b + 1 < nb))
    def _():
        load(b + 1, other).start()

    base = (b * N_EXPERTS + e) * CAP
    group = 8

    def body(c8, carry):
        new = []
        for k in range(group):
            c = c8 * group + k
            dst = pl.multiple_of(idx_ref[base + c] * TOKEN_SUB, TOKEN_SUB)
            src = pl.multiple_of(c * TOKEN_SUB, TOKEN_SUB)
            new.append((dst, acc_ref[slot, pl.ds(dst, TOKEN_SUB), :]
                        + y_ref[0, 0, pl.ds(src, TOKEN_SUB), :] * gate_ref[base + c]))
        for dst, val in new:
            acc_ref[slot, pl.ds(dst, TOKEN_SUB), :] = val
        return carry

    lax.fori_loop(0, CAP // group, body, 0)

    @pl.when(e == N_EXPERTS - 1)
    def _():
        drain(b, slot).start()

    @pl.when((e == N_EXPERTS - 1) & (b == nb - 1))
    def _():
        drain(b, slot).wait()


def _combine_call(idx_flat, gate_flat, y, h3):
    b = h3.shape[0]
    return pl.pallas_call(
        _combine_kernel,
        grid_spec=pltpu.PrefetchScalarGridSpec(
            num_scalar_prefetch=2,
            grid=(b, N_EXPERTS),
            in_specs=[pl.BlockSpec((1, 1, CAP * TOKEN_SUB, LANES), lambda bi, e, idx, gt: (bi, e, 0, 0)),
                      pl.BlockSpec(memory_space=pl.ANY)],
            out_specs=pl.BlockSpec(memory_space=pl.ANY),
            scratch_shapes=[pltpu.VMEM((2, SEQ * TOKEN_SUB, LANES), F32), pltpu.SemaphoreType.DMA((2, 2))],
        ),
        out_shape=jax.ShapeDtypeStruct(h3.shape, F32),
        compiler_params=_cparams(("arbitrary", "arbitrary")),
        name="expert_combine",
    )(idx_flat, gate_flat, y, h3)


def _norm_kernel(x_ref, g_ref, o_ref):
    o_ref[...] = _rms(_load_token_rows(x_ref, ROW_TILE), g_ref[...])


def _norm_call(x3, gain):
    n = x3.shape[0] // TOKEN_SUB
    tm = ROW_TILE
    return pl.pallas_call(
        _norm_kernel,
        grid=(n // tm,),
        in_specs=[pl.BlockSpec((tm * TOKEN_SUB, LANES), lambda i: (i, 0)),
                  pl.BlockSpec((1, D_MODEL), lambda i: (0, 0))],
        out_specs=pl.BlockSpec((tm, D_MODEL), lambda i: (i, 0)),
        out_shape=jax.ShapeDtypeStruct((n, D_MODEL), F32),
        compiler_params=_cparams(("arbitrary",)),
        name="final_norm",
    )(x3, gain)


def _rope_table(dim, lead):
    half = dim // 2
    inv = 1.0 / (ROPE_THETA ** (jnp.arange(0, dim, 2, dtype=F32) / dim))
    ang = jnp.arange(SEQ, dtype=F32)[:, None] * inv[None, :]
    cos, sin = jnp.cos(ang), jnp.sin(ang)
    zero = jnp.zeros_like(sin)
    if lead:
        tail = jnp.zeros((SEQ, LANES - lead - dim), F32)
        ones = jnp.ones((SEQ, lead), F32)
        zl = jnp.zeros((SEQ, lead), F32)
        c = jnp.concatenate([ones, cos, cos, tail], axis=1)
        sa = jnp.concatenate([zl, zero, sin, tail], axis=1)
        sb = jnp.concatenate([zl, -sin, zero, tail], axis=1)
    else:
        reps = LANES // dim
        c = jnp.concatenate([cos, cos] * reps, axis=1)
        sa = jnp.concatenate([zero, sin] * reps, axis=1)
        sb = jnp.concatenate([-sin, zero] * reps, axis=1)
    return jnp.concatenate([c, sa, sb], axis=1)


def _swa_head_perm():
    return [kvh * SWA_GROUP + g for g in range(SWA_GROUP) for kvh in range(SWA_KV_HEADS)]


def _permute_w_in(w):
    offs = np.cumsum([0, A_W, A_W, A_W, MLA_Q_LORA, MLA_KV_LORA, MLA_ROPE, C_W, 2 * HEAD_DIM, 2 * HEAD_DIM])
    a_q, a_k, a_v, b_cq, b_ckv, b_kr, c_q, c_k, c_v = [w[:, offs[i]:offs[i + 1]] for i in range(9)]
    c_q = jnp.concatenate([c_q[:, h * HEAD_DIM:(h + 1) * HEAD_DIM] for h in _swa_head_perm()], axis=1)
    zeros = lambda n: jnp.zeros((w.shape[0], n), w.dtype)
    kr = jnp.concatenate([zeros(MLA_NOPE), b_kr, zeros(LANES - MLA_NOPE - MLA_ROPE)], axis=1)
    return jnp.concatenate([a_q, a_k, a_v, b_cq, b_ckv, c_q, c_k, c_v, kr], axis=1).astype(BF16)


def _permute_mla(w_uq, w_ukv):
    zq = jnp.zeros((MLA_Q_LORA, LANES - MLA_NOPE - MLA_ROPE), w_uq.dtype)
    zk = jnp.zeros((MLA_KV_LORA, LANES - MLA_NOPE), w_ukv.dtype)
    dq = MLA_NOPE + MLA_ROPE
    dkv = MLA_NOPE + MLA_V
    wq = jnp.concatenate([jnp.concatenate([w_uq[:, h * dq:(h + 1) * dq], zq], axis=1)
                          for h in range(MLA_HEADS)], axis=1)
    wk = jnp.concatenate([jnp.concatenate([w_ukv[:, h * dkv:h * dkv + MLA_NOPE], zk], axis=1)
                          for h in range(MLA_HEADS)], axis=1)
    wv = jnp.concatenate([w_ukv[:, h * dkv + MLA_NOPE:(h + 1) * dkv] for h in range(MLA_HEADS)], axis=1)
    return wq.astype(BF16), jnp.concatenate([wk, wv], axis=1).astype(BF16)


def kernel(x, attn_norm, w_in, na_rpb, mla_q_norm, mla_w_uq, mla_kv_norm, mla_w_ukv, swa_sink, group_norm,
           w_out, ffn_norm, w_router, w_gate, w_up, w_down, final_norm):
    bsz, seq, d = x.shape
    assert (seq, d) == (SEQ, D_MODEL)
    n = bsz * seq
    depth = w_in.shape[0]
    rope_s = _rope_table(HEAD_DIM, 0)
    rope_m = _rope_table(MLA_ROPE, MLA_NOPE)
    c_perm = np.concatenate([np.arange(h * HEAD_DIM, (h + 1) * HEAD_DIM) for h in _swa_head_perm()])

    h2 = x.reshape(n, d)
    for l in range(depth):
        tiled = l > 0
        wq, wkv = _permute_mla(mla_w_uq[l], mla_w_ukv[l])
        aq, ak, av, mq, mk, mv, sq, sk, sv = _proj_call(
            h2, attn_norm[l][None], _permute_w_in(w_in[l]), mla_q_norm[l][None], wq, mla_kv_norm[l][None], wkv,
            rope_s, rope_m, tiled)
        r3 = lambda a: a.reshape(bsz, seq, a.shape[-1])
        o_a = _na_call(r3(aq), r3(ak), r3(av), _na_bias_tables(na_rpb[l]))
        o_bt = _mla_call(r3(mq), r3(mk), mv)
        o_c = _swa_call(swa_sink[l], r3(sq), r3(sk), r3(sv))

        gn = group_norm[l]
        wo = w_out[l]
        gc = gn[A_W + B_W:][c_perm]
        wc = wo[A_W + B_W:][c_perm]
        hm, xn3, aff_t = _out_call(
            o_a.reshape(n, A_W), o_bt, o_c.reshape(n, C_W), h2,
            gn[None, :A_W], gn[A_W:A_W + B_W, None], gc[None],
            wo[:A_W].astype(BF16), wo[A_W:A_W + B_W].astype(BF16), wc.astype(BF16),
            ffn_norm[l][None], w_router[l].T, tiled)

        idx, gate = _topk_call(aff_t.reshape(bsz * N_EXPERTS, seq))
        idx_flat = idx.reshape(-1)
        y = _ffn_call(idx_flat, xn3, w_gate, w_up, w_down, l)
        h2 = _combine_call(idx_flat, gate.reshape(-1), y,
                           hm.reshape(bsz, seq * TOKEN_SUB, LANES)).reshape(n * TOKEN_SUB, LANES)
    return _norm_call(h2, final_norm[None]).reshape(bsz, seq, d)
```

```python
import functools

import jax
import jax.numpy as jnp
import numpy as np
from jax import lax
from jax.experimental import pallas as pl
from jax.experimental.pallas import tpu as pltpu

F32 = jnp.float32
BF16 = jnp.bfloat16

D_MODEL = 1024
SEQ = 4096
HEAD_DIM = 64
GRID_W = 64
NA_HEADS = 4
NA_WIN_H = 8
NA_WIN_W = 16
MLA_HEADS = 6
MLA_Q_LORA = 256
MLA_KV_LORA = 128
MLA_NOPE = 64
MLA_ROPE = 32
MLA_V = 64
SWA_HEADS = 6
SWA_KV_HEADS = 2
SWA_GROUP = SWA_HEADS // SWA_KV_HEADS
SWA_WINDOW = 128
ROPE_THETA = 10000.0
N_EXPERTS = 16
EC_CAPACITY = 2
CAP = EC_CAPACITY * SEQ // N_EXPERTS
RMS_EPS = 1e-6
NEG_INF = -1e30

A_W = NA_HEADS * HEAD_DIM
B_W = MLA_HEADS * MLA_V
C_W = SWA_HEADS * HEAD_DIM

LANES = 128
TOKEN_SUB = D_MODEL // LANES
ROW_TILE = 512
VMEM_LIMIT = 56 * 1024 * 1024

P_AQ, P_AK, P_AV = 0, 256, 512
P_CQ, P_CKV = 768, 1024
P_SQ, P_SK, P_SV = 1152, 1536, 1664
P_KR = 1792
P_COLS = 1920

NA_QROWS = 4
NA_KROWS = 12
NA_TQ = NA_QROWS * GRID_W
NA_TK = NA_KROWS * GRID_W
MLA_TQ = 512
MLA_TK = 256
MLA_VT_ROWS = MLA_V + 16
LOG2_E = 1.4426950408889634
FFN_CHUNKS = 4
SWA_TQ = 512
SWA_SUB = 256
SWA_TK = SWA_SUB + 2 * SWA_WINDOW


def _cparams(sem):
    return pltpu.CompilerParams(dimension_semantics=sem, vmem_limit_bytes=VMEM_LIMIT)


def _rms(x, gain):
    ms = jnp.mean(x * x, axis=-1, keepdims=True)
    return x * lax.rsqrt(ms + RMS_EPS) * gain


def _load_token_rows(ref, rows):
    return jnp.concatenate([ref[pl.ds(s, rows, stride=TOKEN_SUB), :] for s in range(TOKEN_SUB)], axis=1)


def _store_token_rows(ref, val):
    rows = val.shape[0]
    for s in range(TOKEN_SUB):
        ref[pl.ds(s, rows, stride=TOKEN_SUB), :] = val[:, s * LANES:(s + 1) * LANES]


def _nt_dot(a, b):
    return lax.dot_general(a, b, (((1,), (1,)), ((), ())), preferred_element_type=F32)


def _proj_kernel(x_ref, gain_ref, win_ref, qn_ref, wq_ref, kvn_ref, wkv_ref, rs_ref, rm_ref,
                 aq_ref, ak_ref, av_ref, mq_ref, mk_ref, mv_ref, sq_ref, sk_ref, sv_ref, *, tiled):
    x = _load_token_rows(x_ref, ROW_TILE) if tiled else x_ref[...]
    xn = _rms(x, gain_ref[...]).astype(BF16)
    proj = jnp.dot(xn, win_ref[...], preferred_element_type=F32)
    lane = lax.broadcasted_iota(jnp.int32, (1, LANES), 1)
    lo_half = lane < HEAD_DIM

    def rope(grp, tab_ref, half):
        c = tab_ref[:, 0:LANES]
        sa = tab_ref[:, LANES:2 * LANES]
        sb = tab_ref[:, 2 * LANES:3 * LANES]
        return grp * c + pltpu.roll(grp, half, 1) * sa + pltpu.roll(grp, LANES - half, 1) * sb

    def split_heads(grp):
        zero = jnp.zeros_like(grp)
        return jnp.where(lo_half, grp, zero).astype(BF16), jnp.where(lo_half, zero, grp).astype(BF16)

    na_scale = HEAD_DIM ** -0.5 * LOG2_E
    for p in range(NA_HEADS // 2):
        grp = proj[:, P_AQ + p * LANES:P_AQ + (p + 1) * LANES] * na_scale
        q0, q1 = split_heads(grp)
        aq_ref[:, (2 * p) * LANES:(2 * p + 1) * LANES] = q0
        aq_ref[:, (2 * p + 1) * LANES:(2 * p + 2) * LANES] = q1
    ak_ref[...] = proj[:, P_AK:P_AK + A_W].astype(BF16)
    av_ref[...] = proj[:, P_AV:P_AV + A_W].astype(BF16)

    cq = _rms(proj[:, P_CQ:P_CQ + MLA_Q_LORA], qn_ref[...]).astype(BF16)
    q = jnp.dot(cq, wq_ref[...], preferred_element_type=F32) * ((MLA_NOPE + MLA_ROPE) ** -0.5 * LOG2_E)
    ckv = _rms(proj[:, P_CKV:P_CKV + MLA_KV_LORA], kvn_ref[...]).astype(BF16)
    kv = jnp.dot(ckv, wkv_ref[...], preferred_element_type=F32)
    kr = rope(proj[:, P_KR:P_KR + LANES], rm_ref, MLA_ROPE // 2)
    for h in range(MLA_HEADS):
        sl = slice(h * LANES, (h + 1) * LANES)
        mq_ref[:, sl] = rope(q[:, sl], rm_ref, MLA_ROPE // 2).astype(BF16)
        mk_ref[:, sl] = (kv[:, sl] + kr).astype(BF16)
    vt = kv[:, MLA_HEADS * LANES:MLA_HEADS * LANES + B_W].T
    pad_rows = MLA_VT_ROWS - MLA_V
    ones_row = jnp.where(lax.broadcasted_iota(jnp.int32, (pad_rows, vt.shape[1]), 0) == 0, 1.0, 0.0)
    for h in range(MLA_HEADS):
        blk = jnp.concatenate([vt[h * MLA_V:(h + 1) * MLA_V], ones_row], axis=0)
        mv_ref[h * MLA_VT_ROWS:(h + 1) * MLA_VT_ROWS, :] = blk.astype(BF16)

    swa_scale = HEAD_DIM ** -0.5 * LOG2_E
    for g in range(SWA_GROUP):
        grp = rope(proj[:, P_SQ + g * LANES:P_SQ + (g + 1) * LANES], rs_ref, HEAD_DIM // 2) * swa_scale
        q0, q1 = split_heads(grp)
        sq_ref[:, (2 * g) * LANES:(2 * g + 1) * LANES] = q0
        sq_ref[:, (2 * g + 1) * LANES:(2 * g + 2) * LANES] = q1
    sk_ref[...] = rope(proj[:, P_SK:P_SK + LANES], rs_ref, HEAD_DIM // 2).astype(BF16)
    sv_ref[...] = proj[:, P_SV:P_SV + LANES].astype(BF16)


def _proj_call(x2, gain, win, qn, wq, kvn, wkv, rope_s, rope_m, tiled):
    tm = ROW_TILE
    n = x2.shape[0] // TOKEN_SUB if tiled else x2.shape[0]
    xspec = pl.BlockSpec((tm * TOKEN_SUB, LANES) if tiled else (tm, D_MODEL), lambda i: (i, 0))
    seq_tiles = SEQ // tm
    row = lambda w: pl.BlockSpec((tm, w), lambda i: (i, 0))
    full = lambda a: pl.BlockSpec(a.shape, lambda i: (0,) * a.ndim)
    pos = lambda w: pl.BlockSpec((tm, w), lambda i: (i % seq_tiles, 0))
    widths = (2 * A_W, A_W, A_W, MLA_HEADS * LANES, MLA_HEADS * LANES, B_W, 2 * C_W, LANES, LANES)
    mv_pos = widths.index(B_W)
    out_specs = [row(w) for w in widths]
    out_shape = [jax.ShapeDtypeStruct((n, w), BF16) for w in widths]
    out_specs[mv_pos] = pl.BlockSpec((MLA_HEADS * MLA_VT_ROWS, tm), lambda i: (0, i))
    out_shape[mv_pos] = jax.ShapeDtypeStruct((MLA_HEADS * MLA_VT_ROWS, n), BF16)
    return pl.pallas_call(
        functools.partial(_proj_kernel, tiled=tiled),
        grid=(n // tm,),
        in_specs=[xspec, full(gain), full(win), full(qn), full(wq), full(kvn), full(wkv),
                  pos(3 * LANES), pos(3 * LANES)],
        out_specs=out_specs,
        out_shape=out_shape,
        compiler_params=_cparams(("arbitrary",)),
        name="norm_in_proj",
    )(x2, gain, win, qn, wq, kvn, wkv, rope_s, rope_m)


def _na_key_start(j):
    rows = SEQ // GRID_W
    return jnp.clip(NA_QROWS * j - NA_WIN_H // 2, 0, rows - NA_KROWS)


def _na_kernel(q_ref, k_ref, v_ref, bias_ref, o_ref):
    j = pl.program_id(1)
    ks = pl.multiple_of(_na_key_start(j) * GRID_W, 256)
    kwin = k_ref[0, pl.ds(ks, NA_TK), :]
    vwin = v_ref[0, pl.ds(ks, NA_TK), :]
    lo_half = lax.broadcasted_iota(jnp.int32, (1, LANES), 1) < HEAD_DIM
    for p in range(NA_HEADS // 2):
        kp = kwin[:, p * LANES:(p + 1) * LANES]
        vp = vwin[:, p * LANES:(p + 1) * LANES]
        outs = []
        for hh in range(2):
            h = 2 * p + hh
            s = _nt_dot(q_ref[0, :, h * LANES:(h + 1) * LANES], kp) + bias_ref[0, h]
            m = jnp.max(s, axis=-1, keepdims=True)
            e = jnp.exp2(s - m)
            l = jnp.sum(e, axis=-1, keepdims=True)
            outs.append(jnp.dot(e.astype(BF16), vp, preferred_element_type=F32) / l)
        o_ref[0, :, p * LANES:(p + 1) * LANES] = jnp.where(lo_half, outs[0], outs[1])


def _na_call(aq, ak, av, bias):
    b = aq.shape[0]
    nj = SEQ // NA_TQ

    def variant(bi, j):
        return (jnp.where(j == 0, 0, jnp.where(j == nj - 1, 2, 1)), 0, 0, 0)

    return pl.pallas_call(
        _na_kernel,
        grid=(b, nj),
        in_specs=[pl.BlockSpec((1, NA_TQ, 2 * A_W), lambda bi, j: (bi, j, 0)),
                  pl.BlockSpec((1, SEQ, A_W), lambda bi, j: (bi, 0, 0)),
                  pl.BlockSpec((1, SEQ, A_W), lambda bi, j: (bi, 0, 0)),
                  pl.BlockSpec((1, NA_HEADS, NA_TQ, NA_TK), variant)],
        out_specs=pl.BlockSpec((1, NA_TQ, A_W), lambda bi, j: (bi, j, 0)),
        out_shape=jax.ShapeDtypeStruct((b, SEQ, A_W), F32),
        compiler_params=_cparams(("arbitrary", "arbitrary")),
        name="neighbourhood_attn",
    )(aq, ak, av, bias)


def _na_bias_tables(rpb):
    rows = SEQ // GRID_W
    nj = SEQ // NA_TQ
    c = np.arange(GRID_W)[:, None]
    kc = np.arange(GRID_W)[None, :]
    cs = np.clip(c - NA_WIN_W // 2, 0, GRID_W - NA_WIN_W)
    cvalid = (kc >= cs) & (kc < cs + NA_WIN_W)
    pad = GRID_W - NA_WIN_W
    padded = jnp.pad(rpb, ((0, 0), (0, 0), (pad, pad)))
    toep = jnp.stack([padded[:, :, GRID_W - 1 - ci:2 * GRID_W - 1 - ci] for ci in range(GRID_W)], axis=2)
    toep = jnp.where(cvalid[None, None], toep * LOG2_E, NEG_INF)
    masked = jnp.full((rpb.shape[0], GRID_W, GRID_W), NEG_INF, F32)
    variants = []
    for j in (0, 1, nj - 1):
        ks = int(np.clip(NA_QROWS * j - NA_WIN_H // 2, 0, rows - NA_KROWS))
        row_blocks = []
        for i in range(NA_QROWS):
            r = NA_QROWS * j + i
            rs = int(np.clip(r - NA_WIN_H // 2, 0, rows - NA_WIN_H))
            blocks = []
            for u in range(NA_KROWS):
                kr = ks + u
                blocks.append(toep[:, kr - r + NA_WIN_H - 1] if rs <= kr < rs + NA_WIN_H else masked)
            row_blocks.append(jnp.concatenate(blocks, axis=2))
        variants.append(jnp.concatenate(row_blocks, axis=1))
    return jnp.stack(variants)


def _mla_kernel(q_ref, k_ref, vt_ref, o_ref):
    heads = range(2)
    nchunks = SEQ // MLA_TK

    def scores(c, hh):
        sl = slice(hh * LANES, (hh + 1) * LANES)
        return _nt_dot(k_ref[0, c * MLA_TK:(c + 1) * MLA_TK, sl], q_ref[0, :, sl])

    m = [jnp.full((1, MLA_TQ), NEG_INF, F32) for _ in heads]
    acc = [jnp.zeros((MLA_VT_ROWS, MLA_TQ), F32) for _ in heads]
    st = [scores(0, hh) for hh in heads]
    for c in range(nchunks):
        st_next = [scores(c + 1, hh) for hh in heads] if c + 1 < nchunks else None
        for hh in heads:
            m_new = jnp.maximum(m[hh], jnp.max(st[hh], axis=0, keepdims=True))
            alpha = jnp.exp2(m[hh] - m_new)
            e = jnp.exp2(st[hh] - m_new).astype(BF16)
            vt = vt_ref[hh * MLA_VT_ROWS:(hh + 1) * MLA_VT_ROWS, c * MLA_TK:(c + 1) * MLA_TK]
            acc[hh] = alpha * acc[hh] + jnp.dot(vt, e, preferred_element_type=F32)
            m[hh] = m_new
        st = st_next
    o_ref[...] = jnp.concatenate([acc[hh][:MLA_V] / acc[hh][MLA_V:MLA_V + 1] for hh in heads], axis=0)


def _mla_call(mq, mk, mvt):
    b = mq.shape[0]
    pairs = MLA_HEADS // 2
    nq = SEQ // MLA_TQ
    return pl.pallas_call(
        _mla_kernel,
        grid=(b, pairs, nq),
        in_specs=[pl.BlockSpec((1, MLA_TQ, 2 * LANES), lambda bi, p, i: (bi, i, p)),
                  pl.BlockSpec((1, SEQ, 2 * LANES), lambda bi, p, i: (bi, 0, p)),
                  pl.BlockSpec((2 * MLA_VT_ROWS, SEQ), lambda bi, p, i: (p, bi))],
        out_specs=pl.BlockSpec((2 * MLA_V, MLA_TQ), lambda bi, p, i: (p, bi * nq + i)),
        out_shape=jax.ShapeDtypeStruct((B_W, b * SEQ), F32),
        compiler_params=_cparams(("arbitrary", "arbitrary", "arbitrary")),
        name="latent_attn",
    )(mq, mk, mvt)


def _swa_kernel(sink_ref, q_ref, k_ref, v_ref, o_ref):
    i = pl.program_id(1)
    lo_half = lax.broadcasted_iota(jnp.int32, (1, LANES), 1) < HEAD_DIM
    for sb in range(SWA_TQ // SWA_SUB):
        rows = slice(sb * SWA_SUB, (sb + 1) * SWA_SUB)
        q0 = SWA_TQ * i + sb * SWA_SUB
        ws = pl.multiple_of(jnp.clip(q0 - SWA_WINDOW, 0, SEQ - SWA_TK), LANES)
        kwin = k_ref[0, pl.ds(ws, SWA_TK), :]
        vwin = v_ref[0, pl.ds(ws, SWA_TK), :]
        qpos = q0 + lax.broadcasted_iota(jnp.int32, (SWA_SUB, 1), 0)
        kpos = ws + lax.broadcasted_iota(jnp.int32, (1, SWA_TK), 1)
        valid = jnp.abs(qpos - kpos) <= SWA_WINDOW
        for g in range(SWA_GROUP):
            outs = []
            for kvh in range(SWA_KV_HEADS):
                grp = 2 * g + kvh
                s = _nt_dot(q_ref[0, rows, grp * LANES:(grp + 1) * LANES], kwin)
                s = jnp.where(valid, s, NEG_INF)
                sink = sink_ref[kvh * SWA_GROUP + g] * LOG2_E
                m = jnp.maximum(jnp.max(s, axis=-1, keepdims=True), sink)
                e = jnp.exp2(s - m)
                l = jnp.sum(e, axis=-1, keepdims=True) + jnp.exp2(sink - m)
                outs.append(jnp.dot(e.astype(BF16), vwin, preferred_element_type=F32) / l)
            o_ref[0, rows, g * LANES:(g + 1) * LANES] = jnp.where(lo_half, outs[0], outs[1])


def _swa_call(sink, sq, sk, sv):
    b = sq.shape[0]
    return pl.pallas_call(
        _swa_kernel,
        grid=(b, SEQ // SWA_TQ),
        in_specs=[pl.BlockSpec(memory_space=pltpu.SMEM),
                  pl.BlockSpec((1, SWA_TQ, 2 * C_W), lambda bi, i: (bi, i, 0)),
                  pl.BlockSpec((1, SEQ, LANES), lambda bi, i: (bi, 0, 0)),
                  pl.BlockSpec((1, SEQ, LANES), lambda bi, i: (bi, 0, 0))],
        out_specs=pl.BlockSpec((1, SWA_TQ, C_W), lambda bi, i: (bi, i, 0)),
        out_shape=jax.ShapeDtypeStruct((b, SEQ, C_W), F32),
        compiler_params=_cparams(("arbitrary", "arbitrary")),
        name="window_gqa",
    )(sink, sq, sk, sv)


def _out_kernel(oa_ref, ob_ref, oc_ref, h_ref, ga_ref, gb_ref, gc_ref, wa_ref, wb_ref, wc_ref,
                gf_ref, wr_ref, hm_ref, xn_ref, aff_ref, *, tiled):
    acc = _load_token_rows(h_ref, ROW_TILE) if tiled else h_ref[...]
    for o_ref, g_ref, w_ref in ((oa_ref, ga_ref, wa_ref), (oc_ref, gc_ref, wc_ref)):
        acc = acc + jnp.dot(_rms(o_ref[...], g_ref[...]).astype(BF16), w_ref[...], preferred_element_type=F32)
    ob = ob_ref[...]
    ms = jnp.mean(ob * ob, axis=0, keepdims=True)
    obn = (ob * lax.rsqrt(ms + RMS_EPS) * gb_ref[...]).astype(BF16)
    acc = acc + lax.dot_general(obn, wb_ref[...], (((0,), (0,)), ((), ())), preferred_element_type=F32)
    _store_token_rows(hm_ref, acc)
    xn = _rms(acc, gf_ref[...])
    _store_token_rows(xn_ref, xn)
    logits = lax.dot_general(wr_ref[...], xn, (((1,), (1,)), ((), ())),
                             precision=lax.Precision.HIGHEST, preferred_element_type=F32)
    m = jnp.max(logits, axis=0, keepdims=True)
    e = jnp.exp(logits - m)
    aff_ref[0] = e / jnp.sum(e, axis=0, keepdims=True)


def _out_call(oa, ob, oc, h2, ga, gb, gc, wa, wb, wc, gf, wr_t, tiled):
    n = oa.shape[0]
    tm = ROW_TILE
    tok = pl.BlockSpec((tm * TOKEN_SUB, LANES), lambda i: (i, 0))
    hspec = tok if tiled else pl.BlockSpec((tm, D_MODEL), lambda i: (i, 0))
    seq_tiles = SEQ // tm
    row = lambda w: pl.BlockSpec((tm, w), lambda i: (i, 0))
    full = lambda a: pl.BlockSpec(a.shape, lambda i: (0,) * a.ndim)
    return pl.pallas_call(
        functools.partial(_out_kernel, tiled=tiled),
        grid=(n // tm,),
        in_specs=[row(A_W), pl.BlockSpec((B_W, tm), lambda i: (0, i)), row(C_W), hspec, full(ga), full(gb), full(gc),
                  full(wa), full(wb), full(wc), full(gf), full(wr_t)],
        out_specs=[tok, tok,
                   pl.BlockSpec((1, N_EXPERTS, tm), lambda i: (i // seq_tiles, 0, i % seq_tiles))],
        out_shape=[jax.ShapeDtypeStruct((n * TOKEN_SUB, LANES), F32), jax.ShapeDtypeStruct((n * TOKEN_SUB, LANES), F32),
                   jax.ShapeDtypeStruct((n // SEQ, N_EXPERTS, SEQ), F32)],
        compiler_params=_cparams(("arbitrary",)),
        name="out_proj_router",
    )(oa, ob, oc, h2, ga, gb, gc, wa, wb, wc, gf, wr_t)


def _topk_kernel(aff_ref, idx_ref, gate_ref, posl_ref, affr_ref):
    rows, seq = aff_ref.shape
    tiles = seq // LANES
    a = aff_ref[...]
    int_min = jnp.int32(-2 ** 31)

    def ordered_to_float(u):
        key = u ^ int_min
        bits = key ^ (lax.shift_right_arithmetic(key, jnp.int32(31)) & jnp.int32(0x7FFFFFFF))
        return lax.bitcast_convert_type(bits, F32)

    t_u = jnp.zeros((rows, 1), jnp.int32)
    for bit in range(31, -1, -1):
        step = int_min if bit == 31 else jnp.int32(1 << bit)
        cand_u = t_u | step
        cnt = jnp.sum(jnp.where(a >= ordered_to_float(cand_u), 1.0, 0.0), axis=1, keepdims=True)
        t_u = jnp.where(cnt >= CAP, cand_u, t_u)
    thr = ordered_to_float(t_u)
    gt = a > thr
    eq = a == thr
    need = CAP - jnp.sum(jnp.where(gt, 1.0, 0.0), axis=1, keepdims=True)

    tri = jnp.where(lax.broadcasted_iota(jnp.int32, (LANES, LANES), 0)
                    <= lax.broadcasted_iota(jnp.int32, (LANES, LANES), 1), 1.0, 0.0).astype(BF16)

    def prefix_incl(flags_f32, t, carry):
        blk = flags_f32[:, t * LANES:(t + 1) * LANES]
        inc = jnp.dot(blk.astype(BF16), tri, preferred_element_type=F32) + carry
        return blk, inc, inc[:, LANES - 1:LANES]

    eq_f = jnp.where(eq, 1.0, 0.0)
    gt_f = jnp.where(gt, 1.0, 0.0)
    carry_eq = jnp.zeros((rows, 1), F32)
    for t in range(tiles):
        eq_blk, eq_inc, carry_eq = prefix_incl(eq_f, t, carry_eq)
        sel_blk = jnp.maximum(gt_f[:, t * LANES:(t + 1) * LANES],
                              jnp.where(eq_inc <= need, eq_blk, 0.0))
        sel_loc = jnp.dot(sel_blk.astype(BF16), tri, preferred_element_type=F32)
        posl_ref[t * rows:(t + 1) * rows, :] = jnp.where(sel_blk > 0.0, sel_loc - 1.0, -1.0)
        affr_ref[t * rows:(t + 1) * rows, :] = a[:, t * LANES:(t + 1) * LANES]

    slot = lax.broadcasted_iota(jnp.int32, (1, CAP), 1).astype(F32)
    tile_id = lax.broadcasted_iota(jnp.int32, (tiles, 1), 0).astype(F32)
    lane_id = lax.broadcasted_iota(jnp.int32, (LANES, 1), 0).astype(F32)
    before = jnp.where(lax.broadcasted_iota(jnp.int32, (tiles, tiles), 1)
                       < lax.broadcasted_iota(jnp.int32, (tiles, tiles), 0), 1.0, 0.0).astype(BF16)
    sub8 = lax.broadcasted_iota(jnp.int32, (8, CAP), 0)
    tn_dot = lambda x, onehot: lax.dot_general(x.astype(BF16), onehot, (((0,), (0,)), ((), ())),
                                               preferred_element_type=F32)

    def row_group(i8, carry):
        r8 = pl.multiple_of(i8 * 8, 8)
        idx8 = jnp.zeros((8, CAP), F32)
        gate8 = jnp.zeros((8, CAP), F32)
        for k in range(8):
            i = r8 + k
            ranks = posl_ref[pl.ds(i, tiles, stride=rows), :]
            affs = affr_ref[pl.ds(i, tiles, stride=rows), :]
            count = jnp.max(ranks, axis=1, keepdims=True) + 1.0
            start = jnp.dot(before, jnp.broadcast_to(count, (tiles, LANES)).astype(BF16),
                            preferred_element_type=F32)[:, 0:1]
            tile_of = jnp.sum(jnp.where(start <= slot, 1.0, 0.0), axis=0, keepdims=True) - 1.0
            pick = tile_id == tile_of
            onehot = jnp.where(pick, 1.0, 0.0).astype(BF16)
            rank = slot - jnp.sum(jnp.where(pick, start, 0.0), axis=0, keepdims=True)
            match = tn_dot(ranks, onehot) == rank
            g0 = affs.astype(BF16).astype(F32)
            r1 = affs - g0
            g1 = r1.astype(BF16).astype(F32)
            vals = tn_dot(g0, onehot) + tn_dot(g1, onehot) + tn_dot(r1 - g1, onehot)
            tok = tile_of * LANES + jnp.sum(jnp.where(match, lane_id, 0.0), axis=0, keepdims=True)
            tok = tok * TOKEN_SUB
            gate = jnp.sum(jnp.where(match, vals, 0.0), axis=0, keepdims=True)
            idx8 = jnp.where(sub8 == k, jnp.broadcast_to(tok, (8, CAP)), idx8)
            gate8 = jnp.where(sub8 == k, jnp.broadcast_to(gate, (8, CAP)), gate8)
        idx_ref[pl.ds(r8, 8), :] = idx8.astype(jnp.int32)
        gate_ref[pl.ds(r8, 8), :] = gate8
        return carry

    lax.fori_loop(0, rows // 8, row_group, 0)


def _topk_call(aff2):
    rows, seq = aff2.shape
    return pl.pallas_call(
        _topk_kernel,
        out_shape=[jax.ShapeDtypeStruct((rows, CAP), jnp.int32), jax.ShapeDtypeStruct((rows, CAP), F32)],
        scratch_shapes=[pltpu.VMEM((seq // LANES * rows, LANES), F32), pltpu.VMEM((seq // LANES * rows, LANES), F32)],
        compiler_params=pltpu.CompilerParams(vmem_limit_bytes=VMEM_LIMIT),
        name="expert_choice_topk",
    )(aff2)


def _ffn_kernel(row_ref, x_hbm, wg_ref, wu_ref, wd_ref, y_ref, xs_ref, wbf_ref, sem_ref):
    e = pl.program_id(0)
    b = pl.program_id(1)
    nb = pl.num_programs(1)
    steps = N_EXPERTS * nb
    t = e * nb + b

    def issue(step, slot):
        sb = step % nb
        base = (sb * N_EXPERTS + step // nb) * CAP
        row0 = sb * (SEQ * TOKEN_SUB)

        def body(c, carry):
            src = pl.multiple_of(row0 + row_ref[base + c], TOKEN_SUB)
            dst = pl.multiple_of(c * TOKEN_SUB, TOKEN_SUB)
            pltpu.make_async_copy(x_hbm.at[pl.ds(src, TOKEN_SUB), :], xs_ref.at[slot, pl.ds(dst, TOKEN_SUB), :],
                                  sem_ref.at[slot]).start()
            return carry

        lax.fori_loop(0, CAP, body, 0, unroll=8)

    @pl.when(t == 0)
    def _():
        issue(t, 0)

    @pl.when(b == 0)
    def _():
        wbf_ref[0] = wg_ref[0, 0].astype(BF16)
        wbf_ref[1] = wu_ref[0, 0].astype(BF16)
        wbf_ref[2] = wd_ref[0, 0].astype(BF16)

    slot = t % 2
    wait_all = lambda s: pltpu.make_async_copy(x_hbm.at[pl.ds(0, CAP * TOKEN_SUB), :], xs_ref.at[s],
                                               sem_ref.at[s]).wait()
    wait_all(slot)
    xs = _load_token_rows(xs_ref.at[slot], CAP).astype(BF16)

    nstep = jnp.where(t + 1 < steps, t + 1, 0)
    nslot = 1 - slot
    nsb = nstep % nb
    nbase = (nsb * N_EXPERTS + nstep // nb) * CAP
    nrow0 = nsb * (SEQ * TOKEN_SUB)
    pieces = 3 * FFN_CHUNKS
    bounds = [CAP * k // pieces for k in range(pieces + 1)]
    piece = iter(range(pieces))

    def issue_piece():
        k = next(piece)
        for c in range(bounds[k], bounds[k + 1]):
            src = pl.multiple_of(nrow0 + row_ref[nbase + c], TOKEN_SUB)
            pltpu.make_async_copy(x_hbm.at[pl.ds(src, TOKEN_SUB), :],
                                  xs_ref.at[nslot, pl.ds(c * TOKEN_SUB, TOKEN_SUB), :], sem_ref.at[nslot]).start()

    width = D_MODEL // FFN_CHUNKS
    y = None
    for j in range(FFN_CHUNKS):
        cols = slice(j * width, (j + 1) * width)
        gate = jnp.dot(xs, wbf_ref[0, :, cols], preferred_element_type=F32)
        issue_piece()
        up = jnp.dot(xs, wbf_ref[1, :, cols], preferred_element_type=F32)
        issue_piece()
        hid = (gate * (1.0 / (1.0 + jnp.exp(-gate))) * up).astype(BF16)
        part = jnp.dot(hid, wbf_ref[2, cols, :], preferred_element_type=F32)
        y = part if y is None else y + part
        issue_piece()
    _store_token_rows(y_ref.at[0, 0], y)

    @pl.when(t == steps - 1)
    def _():
        wait_all(nslot)


def _ffn_call(idx_flat, xn3, wg, wu, wd, layer):
    b = xn3.shape[0] // (SEQ * TOKEN_SUB)
    wspec = pl.BlockSpec((1, 1, D_MODEL, D_MODEL), lambda e, bi, idx: (layer, e, 0, 0))
    return pl.pallas_call(
        _ffn_kernel,
        grid_spec=pltpu.PrefetchScalarGridSpec(
            num_scalar_prefetch=1,
            grid=(N_EXPERTS, b),
            in_specs=[pl.BlockSpec(memory_space=pl.ANY), wspec, wspec, wspec],
            out_specs=pl.BlockSpec((1, 1, CAP * TOKEN_SUB, LANES), lambda e, bi, idx: (bi, e, 0, 0)),
            scratch_shapes=[pltpu.VMEM((2, CAP * TOKEN_SUB, LANES), F32),
                            pltpu.VMEM((3, D_MODEL, D_MODEL), BF16),
                            pltpu.SemaphoreType.DMA((2,))],
        ),
        out_shape=jax.ShapeDtypeStruct((b, N_EXPERTS, CAP * TOKEN_SUB, LANES), F32),
        compiler_params=_cparams(("arbitrary", "arbitrary")),
        name="expert_ffn",
    )(idx_flat, xn3, wg, wu, wd)


def _combine_kernel(row_ref, gate_ref, y_ref, h_hbm, o_hbm, acc_ref, sem_ref):
    b = pl.program_id(0)
    e = pl.program_id(1)
    nb = pl.num_programs(0)
    slot = b % 2
    other = 1 - slot
    load = lambda bb, s: pltpu.make_async_copy(h_hbm.at[bb], acc_ref.at[s], sem_ref.at[0, s])
    drain = lambda bb, s: pltpu.make_async_copy(acc_ref.at[s], o_hbm.at[bb], sem_ref.at[1, s])

    @pl.when((b == 0) & (e == 0))
    def _():
        load(b, slot).start()

    @pl.when(e == 0)
    def _():
        load(b, slot).wait()

    @pl.when((e == 1) & (b >= 1))
    def _():
        drain(b - 1, other).wait()

    @pl.when((e == 1) & (b + 1 < nb))
    def _():
        load(b + 1, other).start()

    base = (b * N_EXPERTS + e) * CAP
    group = 8

    def scatter_add(acc):
        def body(cg, carry):
            first = base + cg * group
            src0 = pl.multiple_of(cg * (group * TOKEN_SUB), group * TOKEN_SUB)
            new = []
            for k in range(group):
                dst = pl.multiple_of(row_ref[first + k], TOKEN_SUB)
                new.append((dst, acc[pl.ds(dst, TOKEN_SUB), :]
                            + y_ref[0, 0, pl.ds(src0 + k * TOKEN_SUB, TOKEN_SUB), :] * gate_ref[first + k]))
            for dst, val in new:
                acc[pl.ds(dst, TOKEN_SUB), :] = val
            return carry

        lax.fori_loop(0, CAP // group, body, 0)

    for s in range(2):
        pl.when(slot == s)(functools.partial(scatter_add, acc_ref.at[s]))

    @pl.when(e == N_EXPERTS - 1)
    def _():
        drain(b, slot).start()

    @pl.when((e == N_EXPERTS - 1) & (b == nb - 1))
    def _():
        drain(b, slot).wait()


def _combine_call(idx_flat, gate_flat, y, h3):
    b = h3.shape[0]
    return pl.pallas_call(
        _combine_kernel,
        grid_spec=pltpu.PrefetchScalarGridSpec(
            num_scalar_prefetch=2,
            grid=(b, N_EXPERTS),
            in_specs=[pl.BlockSpec((1, 1, CAP * TOKEN_SUB, LANES), lambda bi, e, idx, gt: (bi, e, 0, 0)),
                      pl.BlockSpec(memory_space=pl.ANY)],
            out_specs=pl.BlockSpec(memory_space=pl.ANY),
            scratch_shapes=[pltpu.VMEM((2, SEQ * TOKEN_SUB, LANES), F32), pltpu.SemaphoreType.DMA((2, 2))],
        ),
        out_shape=jax.ShapeDtypeStruct(h3.shape, F32),
        compiler_params=_cparams(("arbitrary", "arbitrary")),
        name="expert_combine",
    )(idx_flat, gate_flat, y, h3)


def _norm_kernel(x_ref, g_ref, o_ref):
    o_ref[...] = _rms(_load_token_rows(x_ref, ROW_TILE), g_ref[...])


def _norm_call(x3, gain):
    n = x3.shape[0] // TOKEN_SUB
    tm = ROW_TILE
    return pl.pallas_call(
        _norm_kernel,
        grid=(n // tm,),
        in_specs=[pl.BlockSpec((tm * TOKEN_SUB, LANES), lambda i: (i, 0)),
                  pl.BlockSpec((1, D_MODEL), lambda i: (0, 0))],
        out_specs=pl.BlockSpec((tm, D_MODEL), lambda i: (i, 0)),
        out_shape=jax.ShapeDtypeStruct((n, D_MODEL), F32),
        compiler_params=_cparams(("arbitrary",)),
        name="final_norm",
    )(x3, gain)


def _rope_table(dim, lead):
    half = dim // 2
    inv = 1.0 / (ROPE_THETA ** (jnp.arange(0, dim, 2, dtype=F32) / dim))
    ang = jnp.arange(SEQ, dtype=F32)[:, None] * inv[None, :]
    cos, sin = jnp.cos(ang), jnp.sin(ang)
    zero = jnp.zeros_like(sin)
    if lead:
        tail = jnp.zeros((SEQ, LANES - lead - dim), F32)
        ones = jnp.ones((SEQ, lead), F32)
        zl = jnp.zeros((SEQ, lead), F32)
        c = jnp.concatenate([ones, cos, cos, tail], axis=1)
        sa = jnp.concatenate([zl, zero, sin, tail], axis=1)
        sb = jnp.concatenate([zl, -sin, zero, tail], axis=1)
    else:
        reps = LANES // dim
        c = jnp.concatenate([cos, cos] * reps, axis=1)
        sa = jnp.concatenate([zero, sin] * reps, axis=1)
        sb = jnp.concatenate([-sin, zero] * reps, axis=1)
    return jnp.concatenate([c, sa, sb], axis=1)


def _swa_head_perm():
    return [kvh * SWA_GROUP + g for g in range(SWA_GROUP) for kvh in range(SWA_KV_HEADS)]


def _permute_w_in(w):
    offs = np.cumsum([0, A_W, A_W, A_W, MLA_Q_LORA, MLA_KV_LORA, MLA_ROPE, C_W, 2 * HEAD_DIM, 2 * HEAD_DIM])
    a_q, a_k, a_v, b_cq, b_ckv, b_kr, c_q, c_k, c_v = [w[:, offs[i]:offs[i + 1]] for i in range(9)]
    c_q = jnp.concatenate([c_q[:, h * HEAD_DIM:(h + 1) * HEAD_DIM] for h in _swa_head_perm()], axis=1)
    zeros = lambda n: jnp.zeros((w.shape[0], n), w.dtype)
    kr = jnp.concatenate([zeros(MLA_NOPE), b_kr, zeros(LANES - MLA_NOPE - MLA_ROPE)], axis=1)
    return jnp.concatenate([a_q, a_k, a_v, b_cq, b_ckv, c_q, c_k, c_v, kr], axis=1).astype(BF16)


def _permute_mla(w_uq, w_ukv):
    zq = jnp.zeros((MLA_Q_LORA, LANES - MLA_NOPE - MLA_ROPE), w_uq.dtype)
    zk = jnp.zeros((MLA_KV_LORA, LANES - MLA_NOPE), w_ukv.dtype)
    dq = MLA_NOPE + MLA_ROPE
    dkv = MLA_NOPE + MLA_V
    wq = jnp.concatenate([jnp.concatenate([w_uq[:, h * dq:(h + 1) * dq], zq], axis=1)
                          for h in range(MLA_HEADS)], axis=1)
    wk = jnp.concatenate([jnp.concatenate([w_ukv[:, h * dkv:h * dkv + MLA_NOPE], zk], axis=1)
                          for h in range(MLA_HEADS)], axis=1)
    wv = jnp.concatenate([w_ukv[:, h * dkv + MLA_NOPE:(h + 1) * dkv] for h in range(MLA_HEADS)], axis=1)
    return wq.astype(BF16), jnp.concatenate([wk, wv], axis=1).astype(BF16)


def kernel(x, attn_norm, w_in, na_rpb, mla_q_norm, mla_w_uq, mla_kv_norm, mla_w_ukv, swa_sink, group_norm,
           w_out, ffn_norm, w_router, w_gate, w_up, w_down, final_norm):
    bsz, seq, d = x.shape
    assert (seq, d) == (SEQ, D_MODEL)
    n = bsz * seq
    depth = w_in.shape[0]
    rope_s = _rope_table(HEAD_DIM, 0)
    rope_m = _rope_table(MLA_ROPE, MLA_NOPE)
    c_perm = np.concatenate([np.arange(h * HEAD_DIM, (h + 1) * HEAD_DIM) for h in _swa_head_perm()])

    h2 = x.reshape(n, d)
    for l in range(depth):
        tiled = l > 0
        wq, wkv = _permute_mla(mla_w_uq[l], mla_w_ukv[l])
        aq, ak, av, mq, mk, mv, sq, sk, sv = _proj_call(
            h2, attn_norm[l][None], _permute_w_in(w_in[l]), mla_q_norm[l][None], wq, mla_kv_norm[l][None], wkv,
            rope_s, rope_m, tiled)
        r3 = lambda a: a.reshape(bsz, seq, a.shape[-1])
        o_a = _na_call(r3(aq), r3(ak), r3(av), _na_bias_tables(na_rpb[l]))
        o_bt = _mla_call(r3(mq), r3(mk), mv)
        o_c = _swa_call(swa_sink[l], r3(sq), r3(sk), r3(sv))

        gn = group_norm[l]
        wo = w_out[l]
        gc = gn[A_W + B_W:][c_perm]
        wc = wo[A_W + B_W:][c_perm]
        hm, xn3, aff_t = _out_call(
            o_a.reshape(n, A_W), o_bt, o_c.reshape(n, C_W), h2,
            gn[None, :A_W], gn[A_W:A_W + B_W, None], gc[None],
            wo[:A_W].astype(BF16), wo[A_W:A_W + B_W].astype(BF16), wc.astype(BF16),
            ffn_norm[l][None], w_router[l].T, tiled)

        idx, gate = _topk_call(aff_t.reshape(bsz * N_EXPERTS, seq))
        idx_flat = idx.reshape(-1)
        y = _ffn_call(idx_flat, xn3, w_gate, w_up, w_down, l)
        h2 = _combine_call(idx_flat, gate.reshape(-1), y,
                           hm.reshape(bsz, seq * TOKEN_SUB, LANES)).reshape(n * TOKEN_SUB, LANES)
    return _norm_call(h2, final_norm[None]).reshape(bsz, seq, d)
```

```python
import functools

import jax
import jax.numpy as jnp
import numpy as np
from jax import lax
from jax.experimental import pallas as pl
from jax.experimental.pallas import tpu as pltpu

F32 = jnp.float32
BF16 = jnp.bfloat16

D_MODEL = 1024
SEQ = 4096
HEAD_DIM = 64
GRID_W = 64
NA_HEADS = 4
NA_WIN_H = 8
NA_WIN_W = 16
MLA_HEADS = 6
MLA_Q_LORA = 256
MLA_KV_LORA = 128
MLA_NOPE = 64
MLA_ROPE = 32
MLA_V = 64
SWA_HEADS = 6
SWA_KV_HEADS = 2
SWA_GROUP = SWA_HEADS // SWA_KV_HEADS
SWA_WINDOW = 128
ROPE_THETA = 10000.0
N_EXPERTS = 16
EC_CAPACITY = 2
CAP = EC_CAPACITY * SEQ // N_EXPERTS
RMS_EPS = 1e-6
NEG_INF = -1e30

A_W = NA_HEADS * HEAD_DIM
B_W = MLA_HEADS * MLA_V
C_W = SWA_HEADS * HEAD_DIM

LANES = 128
TOKEN_SUB = D_MODEL // LANES
ROW_TILE = 512
VMEM_LIMIT = 56 * 1024 * 1024

P_AQ, P_AK, P_AV = 0, 256, 512
P_CQ, P_CKV = 768, 1024
P_SQ, P_SK, P_SV = 1152, 1536, 1664
P_KR = 1792
P_COLS = 1920

NA_QROWS = 4
NA_KROWS = 12
NA_TQ = NA_QROWS * GRID_W
NA_TK = NA_KROWS * GRID_W
NA_STEP_BLOCKS = 2
MLA_TQ = 512
MLA_TK = 256
VT_PAD = 16
MLA_VT_ROWS = MLA_V + VT_PAD
VT_ROWS = HEAD_DIM + VT_PAD
LOG2_E = 1.4426950408889634
FFN_CHUNKS = 4
SWA_TQ = 512
SWA_SUB = 256
SWA_TK = SWA_SUB + 2 * SWA_WINDOW


def _cparams(sem):
    return pltpu.CompilerParams(dimension_semantics=sem, vmem_limit_bytes=VMEM_LIMIT)


def _rms(x, gain):
    ms = jnp.mean(x * x, axis=-1, keepdims=True)
    return x * lax.rsqrt(ms + RMS_EPS) * gain


def _load_token_rows(ref, rows):
    return jnp.concatenate([ref[pl.ds(s, rows, stride=TOKEN_SUB), :] for s in range(TOKEN_SUB)], axis=1)


def _store_token_rows(ref, val):
    rows = val.shape[0]
    for s in range(TOKEN_SUB):
        ref[pl.ds(s, rows, stride=TOKEN_SUB), :] = val[:, s * LANES:(s + 1) * LANES]


def _nt_dot(a, b):
    return lax.dot_general(a, b, (((1,), (1,)), ((), ())), preferred_element_type=F32)


def _proj_kernel(x_ref, gain_ref, win_ref, qn_ref, wq_ref, kvn_ref, wkv_ref, rs_ref, rm_ref,
                 aq_ref, ak_ref, av_ref, mq_ref, mk_ref, mv_ref, sq_ref, sk_ref, sv_ref, *, tiled):
    x = _load_token_rows(x_ref, ROW_TILE) if tiled else x_ref[...]
    xn = _rms(x, gain_ref[...]).astype(BF16)
    proj = jnp.dot(xn, win_ref[...], preferred_element_type=F32)
    lane = lax.broadcasted_iota(jnp.int32, (1, LANES), 1)
    lo_half = lane < HEAD_DIM

    def rope(grp, tab_ref, half):
        c = tab_ref[:, 0:LANES]
        sa = tab_ref[:, LANES:2 * LANES]
        sb = tab_ref[:, 2 * LANES:3 * LANES]
        return grp * c + pltpu.roll(grp, half, 1) * sa + pltpu.roll(grp, LANES - half, 1) * sb

    def split_heads(grp):
        zero = jnp.zeros_like(grp)
        return jnp.where(lo_half, grp, zero).astype(BF16), jnp.where(lo_half, zero, grp).astype(BF16)

    def store_values_transposed(ref, vals, heads, width):
        vt = vals.T
        ones_row = jnp.where(lax.broadcasted_iota(jnp.int32, (VT_PAD, vt.shape[1]), 0) == 0, 1.0, 0.0)
        for h in range(heads):
            blk = jnp.concatenate([vt[h * width:(h + 1) * width], ones_row], axis=0)
            ref[h * (width + VT_PAD):(h + 1) * (width + VT_PAD), :] = blk.astype(BF16)

    na_scale = HEAD_DIM ** -0.5 * LOG2_E
    for p in range(NA_HEADS // 2):
        grp = proj[:, P_AQ + p * LANES:P_AQ + (p + 1) * LANES] * na_scale
        q0, q1 = split_heads(grp)
        aq_ref[:, (2 * p) * LANES:(2 * p + 1) * LANES] = q0
        aq_ref[:, (2 * p + 1) * LANES:(2 * p + 2) * LANES] = q1
    ak_ref[...] = proj[:, P_AK:P_AK + A_W].astype(BF16)
    store_values_transposed(av_ref, proj[:, P_AV:P_AV + A_W], NA_HEADS, HEAD_DIM)

    cq = _rms(proj[:, P_CQ:P_CQ + MLA_Q_LORA], qn_ref[...]).astype(BF16)
    q = jnp.dot(cq, wq_ref[...], preferred_element_type=F32) * ((MLA_NOPE + MLA_ROPE) ** -0.5 * LOG2_E)
    ckv = _rms(proj[:, P_CKV:P_CKV + MLA_KV_LORA], kvn_ref[...]).astype(BF16)
    kv = jnp.dot(ckv, wkv_ref[...], preferred_element_type=F32)
    kr = rope(proj[:, P_KR:P_KR + LANES], rm_ref, MLA_ROPE // 2)
    for h in range(MLA_HEADS):
        sl = slice(h * LANES, (h + 1) * LANES)
        mq_ref[:, sl] = rope(q[:, sl], rm_ref, MLA_ROPE // 2).astype(BF16)
        mk_ref[:, sl] = (kv[:, sl] + kr).astype(BF16)
    store_values_transposed(mv_ref, kv[:, MLA_HEADS * LANES:MLA_HEADS * LANES + B_W], MLA_HEADS, MLA_V)

    swa_scale = HEAD_DIM ** -0.5 * LOG2_E
    for g in range(SWA_GROUP):
        grp = rope(proj[:, P_SQ + g * LANES:P_SQ + (g + 1) * LANES], rs_ref, HEAD_DIM // 2) * swa_scale
        q0, q1 = split_heads(grp)
        sq_ref[:, (2 * g) * LANES:(2 * g + 1) * LANES] = q0
        sq_ref[:, (2 * g + 1) * LANES:(2 * g + 2) * LANES] = q1
    sk_ref[...] = rope(proj[:, P_SK:P_SK + LANES], rs_ref, HEAD_DIM // 2).astype(BF16)
    store_values_transposed(sv_ref, proj[:, P_SV:P_SV + LANES], SWA_KV_HEADS, HEAD_DIM)


def _proj_call(x2, gain, win, qn, wq, kvn, wkv, rope_s, rope_m, tiled):
    tm = ROW_TILE
    n = x2.shape[0] // TOKEN_SUB if tiled else x2.shape[0]
    xspec = pl.BlockSpec((tm * TOKEN_SUB, LANES) if tiled else (tm, D_MODEL), lambda i: (i, 0))
    seq_tiles = SEQ // tm
    row = lambda w: pl.BlockSpec((tm, w), lambda i: (i, 0))
    full = lambda a: pl.BlockSpec(a.shape, lambda i: (0,) * a.ndim)
    pos = lambda w: pl.BlockSpec((tm, w), lambda i: (i % seq_tiles, 0))
    outs = ((2 * A_W, False), (A_W, False), (NA_HEADS * VT_ROWS, True),
            (MLA_HEADS * LANES, False), (MLA_HEADS * LANES, False), (MLA_HEADS * MLA_VT_ROWS, True),
            (2 * C_W, False), (LANES, False), (SWA_KV_HEADS * VT_ROWS, True))
    out_specs = [pl.BlockSpec((w, tm), lambda i: (0, i)) if t else row(w) for w, t in outs]
    out_shape = [jax.ShapeDtypeStruct((w, n) if t else (n, w), BF16) for w, t in outs]
    return pl.pallas_call(
        functools.partial(_proj_kernel, tiled=tiled),
        grid=(n // tm,),
        in_specs=[xspec, full(gain), full(win), full(qn), full(wq), full(kvn), full(wkv),
                  pos(3 * LANES), pos(3 * LANES)],
        out_specs=out_specs,
        out_shape=out_shape,
        compiler_params=_cparams(("arbitrary",)),
        name="norm_in_proj",
    )(x2, gain, win, qn, wq, kvn, wkv, rope_s, rope_m)


def _na_key_start(j):
    rows = SEQ // GRID_W
    return jnp.clip(NA_QROWS * j - NA_WIN_H // 2, 0, rows - NA_KROWS)


def _na_kernel(q_ref, k_ref, vt_ref, bias0_ref, bias1_ref, o_ref):
    bias_refs = (bias0_ref, bias1_ref)
    starts, kwins = [], []
    for jj in range(NA_STEP_BLOCKS):
        ks = pl.multiple_of(_na_key_start(NA_STEP_BLOCKS * pl.program_id(1) + jj) * GRID_W, 256)
        starts.append(ks)
        kwins.append(k_ref[0, pl.ds(ks, NA_TK), :])

    def scores(item):
        jj, h = divmod(item, NA_HEADS)
        p = h // 2
        q = q_ref[0, jj * NA_TQ:(jj + 1) * NA_TQ, h * LANES:(h + 1) * LANES]
        return _nt_dot(kwins[jj][:, p * LANES:(p + 1) * LANES], q) + bias_refs[jj][0, h]

    items = NA_STEP_BLOCKS * NA_HEADS
    st = scores(0)
    for item in range(items):
        jj, h = divmod(item, NA_HEADS)
        st_next = scores(item + 1) if item + 1 < items else None
        m = jnp.max(st, axis=0, keepdims=True)
        e = jnp.exp2(st - m).astype(BF16)
        pv = jnp.dot(vt_ref[h * VT_ROWS:(h + 1) * VT_ROWS, pl.ds(starts[jj], NA_TK)], e,
                     preferred_element_type=F32)
        o_ref[h * HEAD_DIM:(h + 1) * HEAD_DIM, jj * NA_TQ:(jj + 1) * NA_TQ] = pv[:HEAD_DIM] / pv[HEAD_DIM:HEAD_DIM + 1]
        st = st_next


def _na_call(aq, ak, avt, bias_t):
    b = aq.shape[0]
    nj = SEQ // NA_TQ
    steps = nj // NA_STEP_BLOCKS

    def variant(jj):
        def index_map(bi, i):
            j = NA_STEP_BLOCKS * i + jj
            return (jnp.where(j == 0, 0, jnp.where(j == nj - 1, 2, 1)), 0, 0, 0)
        return index_map

    bias_spec = lambda jj: pl.BlockSpec((1, NA_HEADS, NA_TK, NA_TQ), variant(jj))
    tq = NA_STEP_BLOCKS * NA_TQ
    return pl.pallas_call(
        _na_kernel,
        grid=(b, steps),
        in_specs=[pl.BlockSpec((1, tq, 2 * A_W), lambda bi, i: (bi, i, 0)),
                  pl.BlockSpec((1, SEQ, A_W), lambda bi, i: (bi, 0, 0)),
                  pl.BlockSpec((NA_HEADS * VT_ROWS, SEQ), lambda bi, i: (0, bi)),
                  bias_spec(0), bias_spec(1)],
        out_specs=pl.BlockSpec((A_W, tq), lambda bi, i: (0, bi * steps + i)),
        out_shape=jax.ShapeDtypeStruct((A_W, b * SEQ), F32),
        compiler_params=_cparams(("arbitrary", "arbitrary")),
        name="neighbourhood_attn",
    )(aq, ak, avt, bias_t, bias_t)


def _na_bias_tables(rpb):
    rows = SEQ // GRID_W
    nj = SEQ // NA_TQ
    c = np.arange(GRID_W)[:, None]
    kc = np.arange(GRID_W)[None, :]
    cs = np.clip(c - NA_WIN_W // 2, 0, GRID_W - NA_WIN_W)
    cvalid = (kc >= cs) & (kc < cs + NA_WIN_W)
    pad = GRID_W - NA_WIN_W
    padded = jnp.pad(rpb, ((0, 0), (0, 0), (pad, pad)))
    toep = jnp.stack([padded[:, :, GRID_W - 1 - ci:2 * GRID_W - 1 - ci] for ci in range(GRID_W)], axis=2)
    toep = jnp.where(cvalid[None, None], toep * LOG2_E, NEG_INF)
    toep_t = jnp.swapaxes(toep, 2, 3)
    masked = jnp.full((rpb.shape[0], GRID_W, GRID_W), NEG_INF, F32)
    variants = []
    for j in (0, 1, nj - 1):
        ks = int(np.clip(NA_QROWS * j - NA_WIN_H // 2, 0, rows - NA_KROWS))
        key_blocks = []
        for u in range(NA_KROWS):
            kr = ks + u
            blocks = []
            for i in range(NA_QROWS):
                r = NA_QROWS * j + i
                rs = int(np.clip(r - NA_WIN_H // 2, 0, rows - NA_WIN_H))
                blocks.append(toep_t[:, kr - r + NA_WIN_H - 1] if rs <= kr < rs + NA_WIN_H else masked)
            key_blocks.append(jnp.concatenate(blocks, axis=2))
        variants.append(jnp.concatenate(key_blocks, axis=1))
    return jnp.stack(variants)


def _mla_kernel(q_ref, k_ref, vt_ref, o_ref):
    heads = range(2)
    nchunks = SEQ // MLA_TK

    def scores(c, hh):
        sl = slice(hh * LANES, (hh + 1) * LANES)
        return _nt_dot(k_ref[0, c * MLA_TK:(c + 1) * MLA_TK, sl], q_ref[0, :, sl])

    m = [jnp.full((1, MLA_TQ), NEG_INF, F32) for _ in heads]
    acc = [jnp.zeros((MLA_VT_ROWS, MLA_TQ), F32) for _ in heads]
    st = [scores(0, hh) for hh in heads]
    for c in range(nchunks):
        st_next = [scores(c + 1, hh) for hh in heads] if c + 1 < nchunks else None
        for hh in heads:
            m_new = jnp.maximum(m[hh], jnp.max(st[hh], axis=0, keepdims=True))
            alpha = jnp.exp2(m[hh] - m_new)
            e = jnp.exp2(st[hh] - m_new).astype(BF16)
            vt = vt_ref[hh * MLA_VT_ROWS:(hh + 1) * MLA_VT_ROWS, c * MLA_TK:(c + 1) * MLA_TK]
            acc[hh] = alpha * acc[hh] + jnp.dot(vt, e, preferred_element_type=F32)
            m[hh] = m_new
        st = st_next
    o_ref[...] = jnp.concatenate([acc[hh][:MLA_V] / acc[hh][MLA_V:MLA_V + 1] for hh in heads], axis=0)


def _mla_call(mq, mk, mvt):
    b = mq.shape[0]
    pairs = MLA_HEADS // 2
    nq = SEQ // MLA_TQ
    return pl.pallas_call(
        _mla_kernel,
        grid=(b, pairs, nq),
        in_specs=[pl.BlockSpec((1, MLA_TQ, 2 * LANES), lambda bi, p, i: (bi, i, p)),
                  pl.BlockSpec((1, SEQ, 2 * LANES), lambda bi, p, i: (bi, 0, p)),
                  pl.BlockSpec((2 * MLA_VT_ROWS, SEQ), lambda bi, p, i: (p, bi))],
        out_specs=pl.BlockSpec((2 * MLA_V, MLA_TQ), lambda bi, p, i: (p, bi * nq + i)),
        out_shape=jax.ShapeDtypeStruct((B_W, b * SEQ), F32),
        compiler_params=_cparams(("arbitrary", "arbitrary", "arbitrary")),
        name="latent_attn",
    )(mq, mk, mvt)


def _swa_kernel(sink_ref, q_ref, k_ref, vt_ref, o_ref):
    i = pl.program_id(1)
    nsub = SWA_TQ // SWA_SUB
    starts, kwins, valids = [], [], []
    for sb in range(nsub):
        q0 = SWA_TQ * i + sb * SWA_SUB
        ws = pl.multiple_of(jnp.clip(q0 - SWA_WINDOW, 0, SEQ - SWA_TK), LANES)
        kpos = ws + lax.broadcasted_iota(jnp.int32, (SWA_TK, 1), 0)
        qpos = q0 + lax.broadcasted_iota(jnp.int32, (1, SWA_SUB), 1)
        starts.append(ws)
        kwins.append(k_ref[0, pl.ds(ws, SWA_TK), :])
        valids.append(jnp.abs(qpos - kpos) <= SWA_WINDOW)

    def scores(item):
        sb, grp = divmod(item, SWA_HEADS)
        st = _nt_dot(kwins[sb], q_ref[0, sb * SWA_SUB:(sb + 1) * SWA_SUB, grp * LANES:(grp + 1) * LANES])
        return jnp.where(valids[sb], st, NEG_INF)

    items = nsub * SWA_HEADS
    st = scores(0)
    for item in range(items):
        sb, grp = divmod(item, SWA_HEADS)
        g, kvh = divmod(grp, SWA_KV_HEADS)
        st_next = scores(item + 1) if item + 1 < items else None
        sink = sink_ref[kvh * SWA_GROUP + g] * LOG2_E
        m = jnp.maximum(jnp.max(st, axis=0, keepdims=True), sink)
        e = jnp.exp2(st - m).astype(BF16)
        pv = jnp.dot(vt_ref[kvh * VT_ROWS:(kvh + 1) * VT_ROWS, pl.ds(starts[sb], SWA_TK)], e,
                     preferred_element_type=F32)
        l = pv[HEAD_DIM:HEAD_DIM + 1] + jnp.exp2(sink - m)
        o_ref[grp * HEAD_DIM:(grp + 1) * HEAD_DIM, sb * SWA_SUB:(sb + 1) * SWA_SUB] = pv[:HEAD_DIM] / l
        st = st_next


def _swa_call(sink, sq, sk, svt):
    b = sq.shape[0]
    ni = SEQ // SWA_TQ
    return pl.pallas_call(
        _swa_kernel,
        grid=(b, ni),
        in_specs=[pl.BlockSpec(memory_space=pltpu.SMEM),
                  pl.BlockSpec((1, SWA_TQ, 2 * C_W), lambda bi, i: (bi, i, 0)),
                  pl.BlockSpec((1, SEQ, LANES), lambda bi, i: (bi, 0, 0)),
                  pl.BlockSpec((SWA_KV_HEADS * VT_ROWS, SEQ), lambda bi, i: (0, bi))],
        out_specs=pl.BlockSpec((C_W, SWA_TQ), lambda bi, i: (0, bi * ni + i)),
        out_shape=jax.ShapeDtypeStruct((C_W, b * SEQ), F32),
        compiler_params=_cparams(("arbitrary", "arbitrary")),
        name="window_gqa",
    )(sink, sq, sk, svt)


def _out_kernel(oa_ref, ob_ref, oc_ref, h_ref, ga_ref, gb_ref, gc_ref, wa_ref, wb_ref, wc_ref,
                gf_ref, wr_ref, hm_ref, xn_ref, aff_ref, *, tiled):
    acc = _load_token_rows(h_ref, ROW_TILE) if tiled else h_ref[...]
    for o_ref, g_ref, w_ref in ((oa_ref, ga_ref, wa_ref), (ob_ref, gb_ref, wb_ref), (oc_ref, gc_ref, wc_ref)):
        o = o_ref[...]
        ms = jnp.mean(o * o, axis=0, keepdims=True)
        on = (o * lax.rsqrt(ms + RMS_EPS) * g_ref[...]).astype(BF16)
        acc = acc + lax.dot_general(on, w_ref[...], (((0,), (0,)), ((), ())), preferred_element_type=F32)
    _store_token_rows(hm_ref, acc)
    xn = _rms(acc, gf_ref[...])
    _store_token_rows(xn_ref, xn)
    logits = lax.dot_general(wr_ref[...], xn, (((1,), (1,)), ((), ())),
                             precision=lax.Precision.HIGHEST, preferred_element_type=F32)
    m = jnp.max(logits, axis=0, keepdims=True)
    e = jnp.exp(logits - m)
    aff_ref[0] = e / jnp.sum(e, axis=0, keepdims=True)


def _out_call(oa, ob, oc, h2, ga, gb, gc, wa, wb, wc, gf, wr_t, tiled):
    n = oa.shape[1]
    tm = ROW_TILE
    col = lambda w: pl.BlockSpec((w, tm), lambda i: (0, i))
    tok = pl.BlockSpec((tm * TOKEN_SUB, LANES), lambda i: (i, 0))
    hspec = tok if tiled else pl.BlockSpec((tm, D_MODEL), lambda i: (i, 0))
    seq_tiles = SEQ // tm
    row = lambda w: pl.BlockSpec((tm, w), lambda i: (i, 0))
    full = lambda a: pl.BlockSpec(a.shape, lambda i: (0,) * a.ndim)
    return pl.pallas_call(
        functools.partial(_out_kernel, tiled=tiled),
        grid=(n // tm,),
        in_specs=[col(A_W), col(B_W), col(C_W), hspec, full(ga), full(gb), full(gc),
                  full(wa), full(wb), full(wc), full(gf), full(wr_t)],
        out_specs=[tok, tok,
                   pl.BlockSpec((1, N_EXPERTS, tm), lambda i: (i // seq_tiles, 0, i % seq_tiles))],
        out_shape=[jax.ShapeDtypeStruct((n * TOKEN_SUB, LANES), F32), jax.ShapeDtypeStruct((n * TOKEN_SUB, LANES), F32),
                   jax.ShapeDtypeStruct((n // SEQ, N_EXPERTS, SEQ), F32)],
        compiler_params=_cparams(("arbitrary",)),
        name="out_proj_router",
    )(oa, ob, oc, h2, ga, gb, gc, wa, wb, wc, gf, wr_t)


def _topk_kernel(aff_ref, idx_ref, gate_ref, posl_ref, affr_ref):
    rows, seq = aff_ref.shape
    tiles = seq // LANES
    a = aff_ref[...]
    int_min = jnp.int32(-2 ** 31)

    def ordered_to_float(u):
        key = u ^ int_min
        bits = key ^ (lax.shift_right_arithmetic(key, jnp.int32(31)) & jnp.int32(0x7FFFFFFF))
        return lax.bitcast_convert_type(bits, F32)

    t_u = jnp.zeros((rows, 1), jnp.int32)
    for bit in range(31, -1, -1):
        step = int_min if bit == 31 else jnp.int32(1 << bit)
        cand_u = t_u | step
        cnt = jnp.sum(jnp.where(a >= ordered_to_float(cand_u), 1.0, 0.0), axis=1, keepdims=True)
        t_u = jnp.where(cnt >= CAP, cand_u, t_u)
    thr = ordered_to_float(t_u)
    gt = a > thr
    eq = a == thr
    need = CAP - jnp.sum(jnp.where(gt, 1.0, 0.0), axis=1, keepdims=True)

    tri = jnp.where(lax.broadcasted_iota(jnp.int32, (LANES, LANES), 0)
                    <= lax.broadcasted_iota(jnp.int32, (LANES, LANES), 1), 1.0, 0.0).astype(BF16)

    def prefix_incl(flags_f32, t, carry):
        blk = flags_f32[:, t * LANES:(t + 1) * LANES]
        inc = jnp.dot(blk.astype(BF16), tri, preferred_element_type=F32) + carry
        return blk, inc, inc[:, LANES - 1:LANES]

    eq_f = jnp.where(eq, 1.0, 0.0)
    gt_f = jnp.where(gt, 1.0, 0.0)
    carry_eq = jnp.zeros((rows, 1), F32)
    for t in range(tiles):
        eq_blk, eq_inc, carry_eq = prefix_incl(eq_f, t, carry_eq)
        sel_blk = jnp.maximum(gt_f[:, t * LANES:(t + 1) * LANES],
                              jnp.where(eq_inc <= need, eq_blk, 0.0))
        sel_loc = jnp.dot(sel_blk.astype(BF16), tri, preferred_element_type=F32)
        posl_ref[t * rows:(t + 1) * rows, :] = jnp.where(sel_blk > 0.0, sel_loc - 1.0, -1.0)
        affr_ref[t * rows:(t + 1) * rows, :] = a[:, t * LANES:(t + 1) * LANES]

    slot = lax.broadcasted_iota(jnp.int32, (1, CAP), 1).astype(F32)
    tile_id = lax.broadcasted_iota(jnp.int32, (tiles, 1), 0).astype(F32)
    lane_id = lax.broadcasted_iota(jnp.int32, (LANES, 1), 0).astype(F32)
    before = jnp.where(lax.broadcasted_iota(jnp.int32, (tiles, tiles), 1)
                       < lax.broadcasted_iota(jnp.int32, (tiles, tiles), 0), 1.0, 0.0).astype(BF16)
    sub8 = lax.broadcasted_iota(jnp.int32, (8, CAP), 0)
    tn_dot = lambda x, onehot: lax.dot_general(x.astype(BF16), onehot, (((0,), (0,)), ((), ())),
                                               preferred_element_type=F32)

    def row_group(i8, carry):
        r8 = pl.multiple_of(i8 * 8, 8)
        idx8 = jnp.zeros((8, CAP), F32)
        gate8 = jnp.zeros((8, CAP), F32)
        for k in range(8):
            i = r8 + k
            ranks = posl_ref[pl.ds(i, tiles, stride=rows), :]
            affs = affr_ref[pl.ds(i, tiles, stride=rows), :]
            count = jnp.max(ranks, axis=1, keepdims=True) + 1.0
            start = jnp.dot(before, jnp.broadcast_to(count, (tiles, LANES)).astype(BF16),
                            preferred_element_type=F32)[:, 0:1]
            tile_of = jnp.sum(jnp.where(start <= slot, 1.0, 0.0), axis=0, keepdims=True) - 1.0
            pick = tile_id == tile_of
            onehot = jnp.where(pick, 1.0, 0.0).astype(BF16)
            rank = slot - jnp.sum(jnp.where(pick, start, 0.0), axis=0, keepdims=True)
            match = tn_dot(ranks, onehot) == rank
            g0 = affs.astype(BF16).astype(F32)
            r1 = affs - g0
            g1 = r1.astype(BF16).astype(F32)
            vals = tn_dot(g0, onehot) + tn_dot(g1, onehot) + tn_dot(r1 - g1, onehot)
            tok = tile_of * LANES + jnp.sum(jnp.where(match, lane_id, 0.0), axis=0, keepdims=True)
            tok = tok * TOKEN_SUB
            gate = jnp.sum(jnp.where(match, vals, 0.0), axis=0, keepdims=True)
            idx8 = jnp.where(sub8 == k, jnp.broadcast_to(tok, (8, CAP)), idx8)
            gate8 = jnp.where(sub8 == k, jnp.broadcast_to(gate, (8, CAP)), gate8)
        idx_ref[pl.ds(r8, 8), :] = idx8.astype(jnp.int32)
        gate_ref[pl.ds(r8, 8), :] = gate8
        return carry

    lax.fori_loop(0, rows // 8, row_group, 0)


def _topk_call(aff2):
    rows, seq = aff2.shape
    return pl.pallas_call(
        _topk_kernel,
        out_shape=[jax.ShapeDtypeStruct((rows, CAP), jnp.int32), jax.ShapeDtypeStruct((rows, CAP), F32)],
        scratch_shapes=[pltpu.VMEM((seq // LANES * rows, LANES), F32), pltpu.VMEM((seq // LANES * rows, LANES), F32)],
        compiler_params=pltpu.CompilerParams(vmem_limit_bytes=VMEM_LIMIT),
        name="expert_choice_topk",
    )(aff2)


def _ffn_kernel(row_ref, x_hbm, wg_ref, wu_ref, wd_ref, y_ref, xs_ref, wbf_ref, sem_ref):
    e = pl.program_id(0)
    b = pl.program_id(1)
    nb = pl.num_programs(1)
    steps = N_EXPERTS * nb
    t = e * nb + b

    def issue(step, slot):
        sb = step % nb
        base = (sb * N_EXPERTS + step // nb) * CAP
        row0 = sb * (SEQ * TOKEN_SUB)

        def body(c, carry):
            src = pl.multiple_of(row0 + row_ref[base + c], TOKEN_SUB)
            dst = pl.multiple_of(c * TOKEN_SUB, TOKEN_SUB)
            pltpu.make_async_copy(x_hbm.at[pl.ds(src, TOKEN_SUB), :], xs_ref.at[slot, pl.ds(dst, TOKEN_SUB), :],
                                  sem_ref.at[slot]).start()
            return carry

        lax.fori_loop(0, CAP, body, 0, unroll=8)

    @pl.when(t == 0)
    def _():
        issue(t, 0)

    @pl.when(b == 0)
    def _():
        wbf_ref[0] = wg_ref[0, 0].astype(BF16)
        wbf_ref[1] = wu_ref[0, 0].astype(BF16)
        wbf_ref[2] = wd_ref[0, 0].astype(BF16)

    slot = t % 2
    wait_all = lambda s: pltpu.make_async_copy(x_hbm.at[pl.ds(0, CAP * TOKEN_SUB), :], xs_ref.at[s],
                                               sem_ref.at[s]).wait()
    wait_all(slot)
    xs = _load_token_rows(xs_ref.at[slot], CAP).astype(BF16)

    nstep = jnp.where(t + 1 < steps, t + 1, 0)
    nslot = 1 - slot
    nsb = nstep % nb
    nbase = (nsb * N_EXPERTS + nstep // nb) * CAP
    nrow0 = nsb * (SEQ * TOKEN_SUB)
    pieces = 3 * FFN_CHUNKS
    bounds = [CAP * k // pieces for k in range(pieces + 1)]
    piece = iter(range(pieces))

    def issue_piece():
        k = next(piece)
        for c in range(bounds[k], bounds[k + 1]):
            src = pl.multiple_of(nrow0 + row_ref[nbase + c], TOKEN_SUB)
            pltpu.make_async_copy(x_hbm.at[pl.ds(src, TOKEN_SUB), :],
                                  xs_ref.at[nslot, pl.ds(c * TOKEN_SUB, TOKEN_SUB), :], sem_ref.at[nslot]).start()

    width = D_MODEL // FFN_CHUNKS
    y = None
    for j in range(FFN_CHUNKS):
        cols = slice(j * width, (j + 1) * width)
        gate = jnp.dot(xs, wbf_ref[0, :, cols], preferred_element_type=F32)
        issue_piece()
        up = jnp.dot(xs, wbf_ref[1, :, cols], preferred_element_type=F32)
        issue_piece()
        hid = (gate * (1.0 / (1.0 + jnp.exp(-gate))) * up).astype(BF16)
        part = jnp.dot(hid, wbf_ref[2, cols, :], preferred_element_type=F32)
        y = part if y is None else y + part
        issue_piece()
    _store_token_rows(y_ref.at[0, 0], y)

    @pl.when(t == steps - 1)
    def _():
        wait_all(nslot)


def _ffn_call(idx_flat, xn3, wg, wu, wd, layer):
    b = xn3.shape[0] // (SEQ * TOKEN_SUB)
    wspec = pl.BlockSpec((1, 1, D_MODEL, D_MODEL), lambda e, bi, idx: (layer, e, 0, 0))
    return pl.pallas_call(
        _ffn_kernel,
        grid_spec=pltpu.PrefetchScalarGridSpec(
            num_scalar_prefetch=1,
            grid=(N_EXPERTS, b),
            in_specs=[pl.BlockSpec(memory_space=pl.ANY), wspec, wspec, wspec],
            out_specs=pl.BlockSpec((1, 1, CAP * TOKEN_SUB, LANES), lambda e, bi, idx: (bi, e, 0, 0)),
            scratch_shapes=[pltpu.VMEM((2, CAP * TOKEN_SUB, LANES), F32),
                            pltpu.VMEM((3, D_MODEL, D_MODEL), BF16),
                            pltpu.SemaphoreType.DMA((2,))],
        ),
        out_shape=jax.ShapeDtypeStruct((b, N_EXPERTS, CAP * TOKEN_SUB, LANES), F32),
        compiler_params=_cparams(("arbitrary", "arbitrary")),
        name="expert_ffn",
    )(idx_flat, xn3, wg, wu, wd)


def _combine_kernel(row_ref, gate_ref, y_ref, h_hbm, o_hbm, acc_ref, sem_ref):
    b = pl.program_id(0)
    e = pl.program_id(1)
    nb = pl.num_programs(0)
    slot = b % 2
    other = 1 - slot
    load = lambda bb, s: pltpu.make_async_copy(h_hbm.at[bb], acc_ref.at[s], sem_ref.at[0, s])
    drain = lambda bb, s: pltpu.make_async_copy(acc_ref.at[s], o_hbm.at[bb], sem_ref.at[1, s])

    @pl.when((b == 0) & (e == 0))
    def _():
        load(b, slot).start()

    @pl.when(e == 0)
    def _():
        load(b, slot).wait()

    @pl.when((e == 1) & (b >= 1))
    def _():
        drain(b - 1, other).wait()

    @pl.when((e == 1) & (b + 1 < nb))
    def _():
        load(b + 1, other).start()

    base = (b * N_EXPERTS + e) * CAP
    group = 8

    def scatter_add(acc):
        def body(cg, carry):
            first = base + cg * group
            src0 = pl.multiple_of(cg * (group * TOKEN_SUB), group * TOKEN_SUB)
            new = []
            for k in range(group):
                dst = pl.multiple_of(row_ref[first + k], TOKEN_SUB)
                new.append((dst, acc[pl.ds(dst, TOKEN_SUB), :]
                            + y_ref[0, 0, pl.ds(src0 + k * TOKEN_SUB, TOKEN_SUB), :] * gate_ref[first + k]))
            for dst, val in new:
                acc[pl.ds(dst, TOKEN_SUB), :] = val
            return carry

        lax.fori_loop(0, CAP // group, body, 0)

    for s in range(2):
        pl.when(slot == s)(functools.partial(scatter_add, acc_ref.at[s]))

    @pl.when(e == N_EXPERTS - 1)
    def _():
        drain(b, slot).start()

    @pl.when((e == N_EXPERTS - 1) & (b == nb - 1))
    def _():
        drain(b, slot).wait()


def _combine_call(idx_flat, gate_flat, y, h3):
    b = h3.shape[0]
    return pl.pallas_call(
        _combine_kernel,
        grid_spec=pltpu.PrefetchScalarGridSpec(
            num_scalar_prefetch=2,
            grid=(b, N_EXPERTS),
            in_specs=[pl.BlockSpec((1, 1, CAP * TOKEN_SUB, LANES), lambda bi, e, idx, gt: (bi, e, 0, 0)),
                      pl.BlockSpec(memory_space=pl.ANY)],
            out_specs=pl.BlockSpec(memory_space=pl.ANY),
            scratch_shapes=[pltpu.VMEM((2, SEQ * TOKEN_SUB, LANES), F32), pltpu.SemaphoreType.DMA((2, 2))],
        ),
        out_shape=jax.ShapeDtypeStruct(h3.shape, F32),
        compiler_params=_cparams(("arbitrary", "arbitrary")),
        name="expert_combine",
    )(idx_flat, gate_flat, y, h3)


def _norm_kernel(x_ref, g_ref, o_ref):
    o_ref[...] = _rms(_load_token_rows(x_ref, ROW_TILE), g_ref[...])


def _norm_call(x3, gain):
    n = x3.shape[0] // TOKEN_SUB
    tm = ROW_TILE
    return pl.pallas_call(
        _norm_kernel,
        grid=(n // tm,),
        in_specs=[pl.BlockSpec((tm * TOKEN_SUB, LANES), lambda i: (i, 0)),
                  pl.BlockSpec((1, D_MODEL), lambda i: (0, 0))],
        out_specs=pl.BlockSpec((tm, D_MODEL), lambda i: (i, 0)),
        out_shape=jax.ShapeDtypeStruct((n, D_MODEL), F32),
        compiler_params=_cparams(("arbitrary",)),
        name="final_norm",
    )(x3, gain)


def _rope_table(dim, lead):
    half = dim // 2
    inv = 1.0 / (ROPE_THETA ** (jnp.arange(0, dim, 2, dtype=F32) / dim))
    ang = jnp.arange(SEQ, dtype=F32)[:, None] * inv[None, :]
    cos, sin = jnp.cos(ang), jnp.sin(ang)
    zero = jnp.zeros_like(sin)
    if lead:
        tail = jnp.zeros((SEQ, LANES - lead - dim), F32)
        ones = jnp.ones((SEQ, lead), F32)
        zl = jnp.zeros((SEQ, lead), F32)
        c = jnp.concatenate([ones, cos, cos, tail], axis=1)
        sa = jnp.concatenate([zl, zero, sin, tail], axis=1)
        sb = jnp.concatenate([zl, -sin, zero, tail], axis=1)
    else:
        reps = LANES // dim
        c = jnp.concatenate([cos, cos] * reps, axis=1)
        sa = jnp.concatenate([zero, sin] * reps, axis=1)
        sb = jnp.concatenate([-sin, zero] * reps, axis=1)
    return jnp.concatenate([c, sa, sb], axis=1)


def _swa_head_perm():
    return [kvh * SWA_GROUP + g for g in range(SWA_GROUP) for kvh in range(SWA_KV_HEADS)]


def _permute_w_in(w):
    offs = np.cumsum([0, A_W, A_W, A_W, MLA_Q_LORA, MLA_KV_LORA, MLA_ROPE, C_W, 2 * HEAD_DIM, 2 * HEAD_DIM])
    a_q, a_k, a_v, b_cq, b_ckv, b_kr, c_q, c_k, c_v = [w[:, offs[i]:offs[i + 1]] for i in range(9)]
    c_q = jnp.concatenate([c_q[:, h * HEAD_DIM:(h + 1) * HEAD_DIM] for h in _swa_head_perm()], axis=1)
    zeros = lambda n: jnp.zeros((w.shape[0], n), w.dtype)
    kr = jnp.concatenate([zeros(MLA_NOPE), b_kr, zeros(LANES - MLA_NOPE - MLA_ROPE)], axis=1)
    return jnp.concatenate([a_q, a_k, a_v, b_cq, b_ckv, c_q, c_k, c_v, kr], axis=1).astype(BF16)


def _permute_mla(w_uq, w_ukv):
    zq = jnp.zeros((MLA_Q_LORA, LANES - MLA_NOPE - MLA_ROPE), w_uq.dtype)
    zk = jnp.zeros((MLA_KV_LORA, LANES - MLA_NOPE), w_ukv.dtype)
    dq = MLA_NOPE + MLA_ROPE
    dkv = MLA_NOPE + MLA_V
    wq = jnp.concatenate([jnp.concatenate([w_uq[:, h * dq:(h + 1) * dq], zq], axis=1)
                          for h in range(MLA_HEADS)], axis=1)
    wk = jnp.concatenate([jnp.concatenate([w_ukv[:, h * dkv:h * dkv + MLA_NOPE], zk], axis=1)
                          for h in range(MLA_HEADS)], axis=1)
    wv = jnp.concatenate([w_ukv[:, h * dkv + MLA_NOPE:(h + 1) * dkv] for h in range(MLA_HEADS)], axis=1)
    return wq.astype(BF16), jnp.concatenate([wk, wv], axis=1).astype(BF16)


def kernel(x, attn_norm, w_in, na_rpb, mla_q_norm, mla_w_uq, mla_kv_norm, mla_w_ukv, swa_sink, group_norm,
           w_out, ffn_norm, w_router, w_gate, w_up, w_down, final_norm):
    bsz, seq, d = x.shape
    assert (seq, d) == (SEQ, D_MODEL)
    n = bsz * seq
    depth = w_in.shape[0]
    rope_s = _rope_table(HEAD_DIM, 0)
    rope_m = _rope_table(MLA_ROPE, MLA_NOPE)
    c_perm = np.concatenate([np.arange(h * HEAD_DIM, (h + 1) * HEAD_DIM) for h in _swa_head_perm()])

    h2 = x.reshape(n, d)
    for l in range(depth):
        tiled = l > 0
        wq, wkv = _permute_mla(mla_w_uq[l], mla_w_ukv[l])
        aq, ak, av, mq, mk, mv, sq, sk, sv = _proj_call(
            h2, attn_norm[l][None], _permute_w_in(w_in[l]), mla_q_norm[l][None], wq, mla_kv_norm[l][None], wkv,
            rope_s, rope_m, tiled)
        r3 = lambda a: a.reshape(bsz, seq, a.shape[-1])
        o_at = _na_call(r3(aq), r3(ak), av, _na_bias_tables(na_rpb[l]))
        o_bt = _mla_call(r3(mq), r3(mk), mv)
        o_ct = _swa_call(swa_sink[l], r3(sq), r3(sk), sv)

        gn = group_norm[l]
        wo = w_out[l]
        gc = gn[A_W + B_W:][c_perm]
        wc = wo[A_W + B_W:][c_perm]
        hm, xn3, aff_t = _out_call(
            o_at, o_bt, o_ct, h2,
            gn[:A_W, None], gn[A_W:A_W + B_W, None], gc[:, None],
            wo[:A_W].astype(BF16), wo[A_W:A_W + B_W].astype(BF16), wc.astype(BF16),
            ffn_norm[l][None], w_router[l].T, tiled)

        idx, gate = _topk_call(aff_t.reshape(bsz * N_EXPERTS, seq))
        idx_flat = idx.reshape(-1)
        y = _ffn_call(idx_flat, xn3, w_gate, w_up, w_down, l)
        h2 = _combine_call(idx_flat, gate.reshape(-1), y,
                           hm.reshape(bsz, seq * TOKEN_SUB, LANES)).reshape(n * TOKEN_SUB, LANES)
    return _norm_call(h2, final_norm[None]).reshape(bsz, seq, d)
```

```python
import functools

import jax
import jax.numpy as jnp
import numpy as np
from jax import lax
from jax.experimental import pallas as pl
from jax.experimental.pallas import tpu as pltpu

F32 = jnp.float32
BF16 = jnp.bfloat16

D_MODEL = 1024
SEQ = 4096
HEAD_DIM = 64
GRID_W = 64
NA_HEADS = 4
NA_WIN_H = 8
NA_WIN_W = 16
MLA_HEADS = 6
MLA_Q_LORA = 256
MLA_KV_LORA = 128
MLA_NOPE = 64
MLA_ROPE = 32
MLA_V = 64
SWA_HEADS = 6
SWA_KV_HEADS = 2
SWA_GROUP = SWA_HEADS // SWA_KV_HEADS
SWA_WINDOW = 128
ROPE_THETA = 10000.0
N_EXPERTS = 16
EC_CAPACITY = 2
CAP = EC_CAPACITY * SEQ // N_EXPERTS
RMS_EPS = 1e-6
NEG_INF = -1e30

A_W = NA_HEADS * HEAD_DIM
B_W = MLA_HEADS * MLA_V
C_W = SWA_HEADS * HEAD_DIM

LANES = 128
TOKEN_SUB = D_MODEL // LANES
ROW_TILE = 512
VMEM_LIMIT = 56 * 1024 * 1024

P_AQ, P_AK, P_AV = 0, 256, 512
P_CQ, P_CKV = 768, 1024
P_SQ, P_SK, P_SV = 1152, 1536, 1664
P_KR = 1792
P_COLS = 1920

NA_QROWS = 4
NA_KROWS = 12
NA_TQ = NA_QROWS * GRID_W
NA_TK = NA_KROWS * GRID_W
NA_STEP_BLOCKS = 2
MLA_TQ = 512
MLA_TK = 256
VT_PAD = 16
MLA_VT_ROWS = MLA_V + VT_PAD
VT_ROWS = HEAD_DIM + VT_PAD
MLA_AHEAD = 3
WIN_AHEAD = 3
LOG2_E = 1.4426950408889634
FFN_CHUNKS = 4
SWA_TQ = 512
SWA_SUB = 256
SWA_TK = SWA_SUB + 2 * SWA_WINDOW


def _cparams(sem):
    return pltpu.CompilerParams(dimension_semantics=sem, vmem_limit_bytes=VMEM_LIMIT)


def _rms(x, gain):
    ms = jnp.mean(x * x, axis=-1, keepdims=True)
    return x * lax.rsqrt(ms + RMS_EPS) * gain


def _load_token_rows(ref, rows):
    return jnp.concatenate([ref[pl.ds(s, rows, stride=TOKEN_SUB), :] for s in range(TOKEN_SUB)], axis=1)


def _store_token_rows(ref, val):
    rows = val.shape[0]
    for s in range(TOKEN_SUB):
        ref[pl.ds(s, rows, stride=TOKEN_SUB), :] = val[:, s * LANES:(s + 1) * LANES]


def _nt_dot(a, b):
    return lax.dot_general(a, b, (((1,), (1,)), ((), ())), preferred_element_type=F32)


def _proj_kernel(x_ref, gain_ref, win_ref, qn_ref, wq_ref, kvn_ref, wkv_ref, rs_ref, rm_ref,
                 aq_ref, ak_ref, av_ref, mq_ref, mk_ref, mv_ref, sq_ref, sk_ref, sv_ref, *, tiled):
    x = _load_token_rows(x_ref, ROW_TILE) if tiled else x_ref[...]
    xn = _rms(x, gain_ref[...]).astype(BF16)
    proj = jnp.dot(xn, win_ref[...], preferred_element_type=F32)
    lane = lax.broadcasted_iota(jnp.int32, (1, LANES), 1)
    lo_half = lane < HEAD_DIM

    def rope(grp, tab_ref, half):
        c = tab_ref[:, 0:LANES]
        sa = tab_ref[:, LANES:2 * LANES]
        sb = tab_ref[:, 2 * LANES:3 * LANES]
        return grp * c + pltpu.roll(grp, half, 1) * sa + pltpu.roll(grp, LANES - half, 1) * sb

    def split_heads(grp):
        zero = jnp.zeros_like(grp)
        return jnp.where(lo_half, grp, zero).astype(BF16), jnp.where(lo_half, zero, grp).astype(BF16)

    def store_values_transposed(ref, vals, heads, width):
        vt = vals.T
        ones_row = jnp.where(lax.broadcasted_iota(jnp.int32, (VT_PAD, vt.shape[1]), 0) == 0, 1.0, 0.0)
        for h in range(heads):
            blk = jnp.concatenate([vt[h * width:(h + 1) * width], ones_row], axis=0)
            ref[h * (width + VT_PAD):(h + 1) * (width + VT_PAD), :] = blk.astype(BF16)

    na_scale = HEAD_DIM ** -0.5 * LOG2_E
    for p in range(NA_HEADS // 2):
        grp = proj[:, P_AQ + p * LANES:P_AQ + (p + 1) * LANES] * na_scale
        q0, q1 = split_heads(grp)
        aq_ref[:, (2 * p) * LANES:(2 * p + 1) * LANES] = q0
        aq_ref[:, (2 * p + 1) * LANES:(2 * p + 2) * LANES] = q1
    ak_ref[...] = proj[:, P_AK:P_AK + A_W].astype(BF16)
    store_values_transposed(av_ref, proj[:, P_AV:P_AV + A_W], NA_HEADS, HEAD_DIM)

    cq = _rms(proj[:, P_CQ:P_CQ + MLA_Q_LORA], qn_ref[...]).astype(BF16)
    q = jnp.dot(cq, wq_ref[...], preferred_element_type=F32) * ((MLA_NOPE + MLA_ROPE) ** -0.5 * LOG2_E)
    ckv = _rms(proj[:, P_CKV:P_CKV + MLA_KV_LORA], kvn_ref[...]).astype(BF16)
    kv = jnp.dot(ckv, wkv_ref[...], preferred_element_type=F32)
    kr = rope(proj[:, P_KR:P_KR + LANES], rm_ref, MLA_ROPE // 2)
    for h in range(MLA_HEADS):
        sl = slice(h * LANES, (h + 1) * LANES)
        mq_ref[:, sl] = rope(q[:, sl], rm_ref, MLA_ROPE // 2).astype(BF16)
        mk_ref[:, sl] = (kv[:, sl] + kr).astype(BF16)
    store_values_transposed(mv_ref, kv[:, MLA_HEADS * LANES:MLA_HEADS * LANES + B_W], MLA_HEADS, MLA_V)

    swa_scale = HEAD_DIM ** -0.5 * LOG2_E
    for g in range(SWA_GROUP):
        grp = rope(proj[:, P_SQ + g * LANES:P_SQ + (g + 1) * LANES], rs_ref, HEAD_DIM // 2) * swa_scale
        q0, q1 = split_heads(grp)
        sq_ref[:, (2 * g) * LANES:(2 * g + 1) * LANES] = q0
        sq_ref[:, (2 * g + 1) * LANES:(2 * g + 2) * LANES] = q1
    sk_ref[...] = rope(proj[:, P_SK:P_SK + LANES], rs_ref, HEAD_DIM // 2).astype(BF16)
    store_values_transposed(sv_ref, proj[:, P_SV:P_SV + LANES], SWA_KV_HEADS, HEAD_DIM)


def _proj_call(x2, gain, win, qn, wq, kvn, wkv, rope_s, rope_m, tiled):
    tm = ROW_TILE
    n = x2.shape[0] // TOKEN_SUB if tiled else x2.shape[0]
    xspec = pl.BlockSpec((tm * TOKEN_SUB, LANES) if tiled else (tm, D_MODEL), lambda i: (i, 0))
    seq_tiles = SEQ // tm
    row = lambda w: pl.BlockSpec((tm, w), lambda i: (i, 0))
    full = lambda a: pl.BlockSpec(a.shape, lambda i: (0,) * a.ndim)
    pos = lambda w: pl.BlockSpec((tm, w), lambda i: (i % seq_tiles, 0))
    outs = ((2 * A_W, False), (A_W, False), (NA_HEADS * VT_ROWS, True),
            (MLA_HEADS * LANES, False), (MLA_HEADS * LANES, False), (MLA_HEADS * MLA_VT_ROWS, True),
            (2 * C_W, False), (LANES, False), (SWA_KV_HEADS * VT_ROWS, True))
    out_specs = [pl.BlockSpec((w, tm), lambda i: (0, i)) if t else row(w) for w, t in outs]
    out_shape = [jax.ShapeDtypeStruct((w, n) if t else (n, w), BF16) for w, t in outs]
    return pl.pallas_call(
        functools.partial(_proj_kernel, tiled=tiled),
        grid=(n // tm,),
        in_specs=[xspec, full(gain), full(win), full(qn), full(wq), full(kvn), full(wkv),
                  pos(3 * LANES), pos(3 * LANES)],
        out_specs=out_specs,
        out_shape=out_shape,
        compiler_params=_cparams(("arbitrary",)),
        name="norm_in_proj",
    )(x2, gain, win, qn, wq, kvn, wkv, rope_s, rope_m)


def _na_key_start(j):
    rows = SEQ // GRID_W
    return jnp.clip(NA_QROWS * j - NA_WIN_H // 2, 0, rows - NA_KROWS)


def _na_kernel(q_ref, k_ref, vt_ref, bias0_ref, bias1_ref, o_ref):
    bias_refs = (bias0_ref, bias1_ref)
    starts, kwins = [], []
    for jj in range(NA_STEP_BLOCKS):
        ks = pl.multiple_of(_na_key_start(NA_STEP_BLOCKS * pl.program_id(1) + jj) * GRID_W, 256)
        starts.append(ks)
        kwins.append(k_ref[0, pl.ds(ks, NA_TK), :])

    def scores(item):
        jj, h = divmod(item, NA_HEADS)
        p = h // 2
        q = q_ref[0, jj * NA_TQ:(jj + 1) * NA_TQ, h * LANES:(h + 1) * LANES]
        return _nt_dot(kwins[jj][:, p * LANES:(p + 1) * LANES], q) + bias_refs[jj][0, h]

    items = NA_STEP_BLOCKS * NA_HEADS
    ahead = [scores(it) for it in range(WIN_AHEAD)]
    for item in range(items):
        jj, h = divmod(item, NA_HEADS)
        st = ahead.pop(0)
        if item + WIN_AHEAD < items:
            ahead.append(scores(item + WIN_AHEAD))
        m = jnp.max(st, axis=0, keepdims=True)
        e = jnp.exp2(st - m).astype(BF16)
        pv = jnp.dot(vt_ref[h * VT_ROWS:(h + 1) * VT_ROWS, pl.ds(starts[jj], NA_TK)], e,
                     preferred_element_type=F32)
        out = pv[:HEAD_DIM] / pv[HEAD_DIM:HEAD_DIM + 1]
        o_ref[h * HEAD_DIM:(h + 1) * HEAD_DIM, jj * NA_TQ:(jj + 1) * NA_TQ] = out.astype(o_ref.dtype)


def _na_call(aq, ak, avt, bias_t):
    b = aq.shape[0]
    nj = SEQ // NA_TQ
    steps = nj // NA_STEP_BLOCKS

    def variant(jj):
        def index_map(bi, i):
            j = NA_STEP_BLOCKS * i + jj
            return (jnp.where(j == 0, 0, jnp.where(j == nj - 1, 2, 1)), 0, 0, 0)
        return index_map

    bias_spec = lambda jj: pl.BlockSpec((1, NA_HEADS, NA_TK, NA_TQ), variant(jj))
    tq = NA_STEP_BLOCKS * NA_TQ
    return pl.pallas_call(
        _na_kernel,
        grid=(b, steps),
        in_specs=[pl.BlockSpec((1, tq, 2 * A_W), lambda bi, i: (bi, i, 0)),
                  pl.BlockSpec((1, SEQ, A_W), lambda bi, i: (bi, 0, 0)),
                  pl.BlockSpec((NA_HEADS * VT_ROWS, SEQ), lambda bi, i: (0, bi)),
                  bias_spec(0), bias_spec(1)],
        out_specs=pl.BlockSpec((A_W, tq), lambda bi, i: (0, bi * steps + i)),
        out_shape=jax.ShapeDtypeStruct((A_W, b * SEQ), BF16),
        compiler_params=_cparams(("arbitrary", "arbitrary")),
        name="neighbourhood_attn",
    )(aq, ak, avt, bias_t, bias_t)


def _na_bias_tables(rpb):
    rows = SEQ // GRID_W
    nj = SEQ // NA_TQ
    c = np.arange(GRID_W)[:, None]
    kc = np.arange(GRID_W)[None, :]
    cs = np.clip(c - NA_WIN_W // 2, 0, GRID_W - NA_WIN_W)
    cvalid = (kc >= cs) & (kc < cs + NA_WIN_W)
    pad = GRID_W - NA_WIN_W
    padded = jnp.pad(rpb, ((0, 0), (0, 0), (pad, pad)))
    toep = jnp.stack([padded[:, :, GRID_W - 1 - ci:2 * GRID_W - 1 - ci] for ci in range(GRID_W)], axis=2)
    toep = jnp.where(cvalid[None, None], toep * LOG2_E, NEG_INF)
    toep_t = jnp.swapaxes(toep, 2, 3)
    masked = jnp.full((rpb.shape[0], GRID_W, GRID_W), NEG_INF, F32)
    variants = []
    for j in (0, 1, nj - 1):
        ks = int(np.clip(NA_QROWS * j - NA_WIN_H // 2, 0, rows - NA_KROWS))
        key_blocks = []
        for u in range(NA_KROWS):
            kr = ks + u
            blocks = []
            for i in range(NA_QROWS):
                r = NA_QROWS * j + i
                rs = int(np.clip(r - NA_WIN_H // 2, 0, rows - NA_WIN_H))
                blocks.append(toep_t[:, kr - r + NA_WIN_H - 1] if rs <= kr < rs + NA_WIN_H else masked)
            key_blocks.append(jnp.concatenate(blocks, axis=2))
        variants.append(jnp.concatenate(key_blocks, axis=1))
    return jnp.stack(variants)


def _mla_kernel(q_ref, k_ref, vt_ref, o_ref):
    heads = range(2)
    nchunks = SEQ // MLA_TK

    def scores(c, hh):
        sl = slice(hh * LANES, (hh + 1) * LANES)
        return _nt_dot(k_ref[0, c * MLA_TK:(c + 1) * MLA_TK, sl], q_ref[0, :, sl])

    m = [jnp.full((1, MLA_TQ), NEG_INF, F32) for _ in heads]
    acc = [jnp.zeros((MLA_VT_ROWS, MLA_TQ), F32) for _ in heads]
    items = [(c, hh) for c in range(nchunks) for hh in heads]
    ahead = [scores(*it) for it in items[:MLA_AHEAD]]
    for n, (c, hh) in enumerate(items):
        st = ahead.pop(0)
        if n + MLA_AHEAD < len(items):
            ahead.append(scores(*items[n + MLA_AHEAD]))
        m_new = jnp.maximum(m[hh], jnp.max(st, axis=0, keepdims=True))
        alpha = jnp.exp2(m[hh] - m_new)
        e = jnp.exp2(st - m_new).astype(BF16)
        vt = vt_ref[hh * MLA_VT_ROWS:(hh + 1) * MLA_VT_ROWS, c * MLA_TK:(c + 1) * MLA_TK]
        acc[hh] = alpha * acc[hh] + jnp.dot(vt, e, preferred_element_type=F32)
        m[hh] = m_new
    out = jnp.concatenate([acc[hh][:MLA_V] / acc[hh][MLA_V:MLA_V + 1] for hh in heads], axis=0)
    o_ref[...] = out.astype(o_ref.dtype)


def _mla_call(mq, mk, mvt):
    b = mq.shape[0]
    pairs = MLA_HEADS // 2
    nq = SEQ // MLA_TQ
    return pl.pallas_call(
        _mla_kernel,
        grid=(b, pairs, nq),
        in_specs=[pl.BlockSpec((1, MLA_TQ, 2 * LANES), lambda bi, p, i: (bi, i, p)),
                  pl.BlockSpec((1, SEQ, 2 * LANES), lambda bi, p, i: (bi, 0, p)),
                  pl.BlockSpec((2 * MLA_VT_ROWS, SEQ), lambda bi, p, i: (p, bi))],
        out_specs=pl.BlockSpec((2 * MLA_V, MLA_TQ), lambda bi, p, i: (p, bi * nq + i)),
        out_shape=jax.ShapeDtypeStruct((B_W, b * SEQ), BF16),
        compiler_params=_cparams(("arbitrary", "arbitrary", "arbitrary")),
        name="latent_attn",
    )(mq, mk, mvt)


def _swa_kernel(sink_ref, q_ref, k_ref, vt_ref, o_ref):
    i = pl.program_id(1)
    nsub = SWA_TQ // SWA_SUB
    starts, kwins, valids = [], [], []
    for sb in range(nsub):
        q0 = SWA_TQ * i + sb * SWA_SUB
        ws = pl.multiple_of(jnp.clip(q0 - SWA_WINDOW, 0, SEQ - SWA_TK), LANES)
        kpos = ws + lax.broadcasted_iota(jnp.int32, (SWA_TK, 1), 0)
        qpos = q0 + lax.broadcasted_iota(jnp.int32, (1, SWA_SUB), 1)
        starts.append(ws)
        kwins.append(k_ref[0, pl.ds(ws, SWA_TK), :])
        valids.append(jnp.abs(qpos - kpos) <= SWA_WINDOW)

    def scores(item):
        sb, grp = divmod(item, SWA_HEADS)
        st = _nt_dot(kwins[sb], q_ref[0, sb * SWA_SUB:(sb + 1) * SWA_SUB, grp * LANES:(grp + 1) * LANES])
        return jnp.where(valids[sb], st, NEG_INF)

    items = nsub * SWA_HEADS
    ahead = [scores(it) for it in range(WIN_AHEAD)]
    for item in range(items):
        sb, grp = divmod(item, SWA_HEADS)
        g, kvh = divmod(grp, SWA_KV_HEADS)
        st = ahead.pop(0)
        if item + WIN_AHEAD < items:
            ahead.append(scores(item + WIN_AHEAD))
        sink = sink_ref[kvh * SWA_GROUP + g] * LOG2_E
        m = jnp.maximum(jnp.max(st, axis=0, keepdims=True), sink)
        e = jnp.exp2(st - m).astype(BF16)
        pv = jnp.dot(vt_ref[kvh * VT_ROWS:(kvh + 1) * VT_ROWS, pl.ds(starts[sb], SWA_TK)], e,
                     preferred_element_type=F32)
        l = pv[HEAD_DIM:HEAD_DIM + 1] + jnp.exp2(sink - m)
        out = pv[:HEAD_DIM] / l
        o_ref[grp * HEAD_DIM:(grp + 1) * HEAD_DIM, sb * SWA_SUB:(sb + 1) * SWA_SUB] = out.astype(o_ref.dtype)


def _swa_call(sink, sq, sk, svt):
    b = sq.shape[0]
    ni = SEQ // SWA_TQ
    return pl.pallas_call(
        _swa_kernel,
        grid=(b, ni),
        in_specs=[pl.BlockSpec(memory_space=pltpu.SMEM),
                  pl.BlockSpec((1, SWA_TQ, 2 * C_W), lambda bi, i: (bi, i, 0)),
                  pl.BlockSpec((1, SEQ, LANES), lambda bi, i: (bi, 0, 0)),
                  pl.BlockSpec((SWA_KV_HEADS * VT_ROWS, SEQ), lambda bi, i: (0, bi))],
        out_specs=pl.BlockSpec((C_W, SWA_TQ), lambda bi, i: (0, bi * ni + i)),
        out_shape=jax.ShapeDtypeStruct((C_W, b * SEQ), BF16),
        compiler_params=_cparams(("arbitrary", "arbitrary")),
        name="window_gqa",
    )(sink, sq, sk, svt)


def _out_kernel(oa_ref, ob_ref, oc_ref, h_ref, ga_ref, gb_ref, gc_ref, wa_ref, wb_ref, wc_ref,
                gf_ref, wr_ref, hm_ref, xn_ref, aff_ref, *, tiled):
    acc = _load_token_rows(h_ref, ROW_TILE) if tiled else h_ref[...]
    for o_ref, g_ref, w_ref in ((oa_ref, ga_ref, wa_ref), (ob_ref, gb_ref, wb_ref), (oc_ref, gc_ref, wc_ref)):
        o = o_ref[...].astype(F32)
        ms = jnp.mean(o * o, axis=0, keepdims=True)
        on = (o * lax.rsqrt(ms + RMS_EPS) * g_ref[...]).astype(BF16)
        acc = acc + lax.dot_general(on, w_ref[...], (((0,), (0,)), ((), ())), preferred_element_type=F32)
    _store_token_rows(hm_ref, acc)
    xn = _rms(acc, gf_ref[...])
    _store_token_rows(xn_ref, xn)
    wr = wr_ref[...]
    wr_hi = wr.astype(BF16)
    wr_lo = (wr - wr_hi.astype(F32)).astype(BF16)
    xn_hi = xn.astype(BF16)
    xn_lo = (xn - xn_hi.astype(F32)).astype(BF16)
    logits = _nt_dot(wr_hi, xn_hi) + (_nt_dot(wr_hi, xn_lo) + _nt_dot(wr_lo, xn_hi))
    m = jnp.max(logits, axis=0, keepdims=True)
    e = jnp.exp(logits - m)
    aff_ref[0] = e / jnp.sum(e, axis=0, keepdims=True)


def _out_call(oa, ob, oc, h2, ga, gb, gc, wa, wb, wc, gf, wr_t, tiled):
    n = oa.shape[1]
    tm = ROW_TILE
    col = lambda w: pl.BlockSpec((w, tm), lambda i: (0, i))
    tok = pl.BlockSpec((tm * TOKEN_SUB, LANES), lambda i: (i, 0))
    hspec = tok if tiled else pl.BlockSpec((tm, D_MODEL), lambda i: (i, 0))
    seq_tiles = SEQ // tm
    row = lambda w: pl.BlockSpec((tm, w), lambda i: (i, 0))
    full = lambda a: pl.BlockSpec(a.shape, lambda i: (0,) * a.ndim)
    return pl.pallas_call(
        functools.partial(_out_kernel, tiled=tiled),
        grid=(n // tm,),
        in_specs=[col(A_W), col(B_W), col(C_W), hspec, full(ga), full(gb), full(gc),
                  full(wa), full(wb), full(wc), full(gf), full(wr_t)],
        out_specs=[tok, tok,
                   pl.BlockSpec((1, N_EXPERTS, tm), lambda i: (i // seq_tiles, 0, i % seq_tiles))],
        out_shape=[jax.ShapeDtypeStruct((n * TOKEN_SUB, LANES), F32), jax.ShapeDtypeStruct((n * TOKEN_SUB, LANES), F32),
                   jax.ShapeDtypeStruct((n // SEQ, N_EXPERTS, SEQ), F32)],
        compiler_params=_cparams(("arbitrary",)),
        name="out_proj_router",
    )(oa, ob, oc, h2, ga, gb, gc, wa, wb, wc, gf, wr_t)


def _topk_kernel(aff_ref, idx_ref, gate_ref, posl_ref, affr_ref):
    rows, seq = aff_ref.shape
    tiles = seq // LANES
    a = aff_ref[...]
    int_min = jnp.int32(-2 ** 31)

    def ordered_to_float(u):
        key = u ^ int_min
        bits = key ^ (lax.shift_right_arithmetic(key, jnp.int32(31)) & jnp.int32(0x7FFFFFFF))
        return lax.bitcast_convert_type(bits, F32)

    t_u = jnp.zeros((rows, 1), jnp.int32)
    for bit in range(31, -1, -1):
        step = int_min if bit == 31 else jnp.int32(1 << bit)
        cand_u = t_u | step
        cnt = jnp.sum(jnp.where(a >= ordered_to_float(cand_u), 1.0, 0.0), axis=1, keepdims=True)
        t_u = jnp.where(cnt >= CAP, cand_u, t_u)
    thr = ordered_to_float(t_u)
    gt = a > thr
    eq = a == thr
    need = CAP - jnp.sum(jnp.where(gt, 1.0, 0.0), axis=1, keepdims=True)

    tri = jnp.where(lax.broadcasted_iota(jnp.int32, (LANES, LANES), 0)
                    <= lax.broadcasted_iota(jnp.int32, (LANES, LANES), 1), 1.0, 0.0).astype(BF16)

    def prefix_incl(flags_f32, t, carry):
        blk = flags_f32[:, t * LANES:(t + 1) * LANES]
        inc = jnp.dot(blk.astype(BF16), tri, preferred_element_type=F32) + carry
        return blk, inc, inc[:, LANES - 1:LANES]

    eq_f = jnp.where(eq, 1.0, 0.0)
    gt_f = jnp.where(gt, 1.0, 0.0)
    carry_eq = jnp.zeros((rows, 1), F32)
    for t in range(tiles):
        eq_blk, eq_inc, carry_eq = prefix_incl(eq_f, t, carry_eq)
        sel_blk = jnp.maximum(gt_f[:, t * LANES:(t + 1) * LANES],
                              jnp.where(eq_inc <= need, eq_blk, 0.0))
        sel_loc = jnp.dot(sel_blk.astype(BF16), tri, preferred_element_type=F32)
        posl_ref[t * rows:(t + 1) * rows, :] = jnp.where(sel_blk > 0.0, sel_loc - 1.0, -1.0)
        affr_ref[t * rows:(t + 1) * rows, :] = a[:, t * LANES:(t + 1) * LANES]

    slot = lax.broadcasted_iota(jnp.int32, (1, CAP), 1).astype(F32)
    tile_id = lax.broadcasted_iota(jnp.int32, (tiles, 1), 0).astype(F32)
    lane_id = lax.broadcasted_iota(jnp.int32, (LANES, 1), 0).astype(F32)
    before = jnp.where(lax.broadcasted_iota(jnp.int32, (tiles, tiles), 1)
                       < lax.broadcasted_iota(jnp.int32, (tiles, tiles), 0), 1.0, 0.0).astype(BF16)
    sub8 = lax.broadcasted_iota(jnp.int32, (8, CAP), 0)
    tn_dot = lambda x, onehot: lax.dot_general(x.astype(BF16), onehot, (((0,), (0,)), ((), ())),
                                               preferred_element_type=F32)

    def row_group(i8, carry):
        r8 = pl.multiple_of(i8 * 8, 8)
        idx8 = jnp.zeros((8, CAP), F32)
        gate8 = jnp.zeros((8, CAP), F32)
        for k in range(8):
            i = r8 + k
            ranks = posl_ref[pl.ds(i, tiles, stride=rows), :]
            affs = affr_ref[pl.ds(i, tiles, stride=rows), :]
            count = jnp.max(ranks, axis=1, keepdims=True) + 1.0
            start = jnp.dot(before, jnp.broadcast_to(count, (tiles, LANES)).astype(BF16),
                            preferred_element_type=F32)[:, 0:1]
            tile_of = jnp.sum(jnp.where(start <= slot, 1.0, 0.0), axis=0, keepdims=True) - 1.0
            pick = tile_id == tile_of
            onehot = jnp.where(pick, 1.0, 0.0).astype(BF16)
            rank = slot - jnp.sum(jnp.where(pick, start, 0.0), axis=0, keepdims=True)
            match = tn_dot(ranks, onehot) == rank
            g0 = affs.astype(BF16).astype(F32)
            r1 = affs - g0
            g1 = r1.astype(BF16).astype(F32)
            vals = tn_dot(g0, onehot) + tn_dot(g1, onehot) + tn_dot(r1 - g1, onehot)
            tok = tile_of * LANES + jnp.sum(jnp.where(match, lane_id, 0.0), axis=0, keepdims=True)
            tok = tok * TOKEN_SUB
            gate = jnp.sum(jnp.where(match, vals, 0.0), axis=0, keepdims=True)
            idx8 = jnp.where(sub8 == k, jnp.broadcast_to(tok, (8, CAP)), idx8)
            gate8 = jnp.where(sub8 == k, jnp.broadcast_to(gate, (8, CAP)), gate8)
        idx_ref[pl.ds(r8, 8), :] = idx8.astype(jnp.int32)
        gate_ref[pl.ds(r8, 8), :] = gate8
        return carry

    lax.fori_loop(0, rows // 8, row_group, 0)


def _topk_call(aff2):
    rows, seq = aff2.shape
    return pl.pallas_call(
        _topk_kernel,
        out_shape=[jax.ShapeDtypeStruct((rows, CAP), jnp.int32), jax.ShapeDtypeStruct((rows, CAP), F32)],
        scratch_shapes=[pltpu.VMEM((seq // LANES * rows, LANES), F32), pltpu.VMEM((seq // LANES * rows, LANES), F32)],
        compiler_params=pltpu.CompilerParams(vmem_limit_bytes=VMEM_LIMIT),
        name="expert_choice_topk",
    )(aff2)


def _ffn_kernel(row_ref, x_hbm, wg_ref, wu_ref, wd_ref, y_ref, xs_ref, wbf_ref, sem_ref):
    e = pl.program_id(0)
    b = pl.program_id(1)
    nb = pl.num_programs(1)
    steps = N_EXPERTS * nb
    t = e * nb + b

    def issue(step, slot):
        sb = step % nb
        base = (sb * N_EXPERTS + step // nb) * CAP
        row0 = sb * (SEQ * TOKEN_SUB)

        def body(c, carry):
            src = pl.multiple_of(row0 + row_ref[base + c], TOKEN_SUB)
            dst = pl.multiple_of(c * TOKEN_SUB, TOKEN_SUB)
            pltpu.make_async_copy(x_hbm.at[pl.ds(src, TOKEN_SUB), :], xs_ref.at[slot, pl.ds(dst, TOKEN_SUB), :],
                                  sem_ref.at[slot]).start()
            return carry

        lax.fori_loop(0, CAP, body, 0, unroll=8)

    @pl.when(t == 0)
    def _():
        issue(t, 0)

    @pl.when(b == 0)
    def _():
        wbf_ref[0] = wg_ref[0, 0].astype(BF16)
        wbf_ref[1] = wu_ref[0, 0].astype(BF16)
        wbf_ref[2] = wd_ref[0, 0].astype(BF16)

    slot = t % 2
    wait_all = lambda s: pltpu.make_async_copy(x_hbm.at[pl.ds(0, CAP * TOKEN_SUB), :], xs_ref.at[s],
                                               sem_ref.at[s]).wait()
    wait_all(slot)
    xs = _load_token_rows(xs_ref.at[slot], CAP).astype(BF16)

    nstep = jnp.where(t + 1 < steps, t + 1, 0)
    nslot = 1 - slot
    nsb = nstep % nb
    nbase = (nsb * N_EXPERTS + nstep // nb) * CAP
    nrow0 = nsb * (SEQ * TOKEN_SUB)
    pieces = 3 * FFN_CHUNKS
    bounds = [CAP * k // pieces for k in range(pieces + 1)]
    piece = iter(range(pieces))

    def issue_piece():
        k = next(piece)
        for c in range(bounds[k], bounds[k + 1]):
            src = pl.multiple_of(nrow0 + row_ref[nbase + c], TOKEN_SUB)
            pltpu.make_async_copy(x_hbm.at[pl.ds(src, TOKEN_SUB), :],
                                  xs_ref.at[nslot, pl.ds(c * TOKEN_SUB, TOKEN_SUB), :], sem_ref.at[nslot]).start()

    width = D_MODEL // FFN_CHUNKS
    y = None
    for j in range(FFN_CHUNKS):
        cols = slice(j * width, (j + 1) * width)
        gate = jnp.dot(xs, wbf_ref[0, :, cols], preferred_element_type=F32)
        issue_piece()
        up = jnp.dot(xs, wbf_ref[1, :, cols], preferred_element_type=F32)
        issue_piece()
        hid = (gate * (1.0 / (1.0 + jnp.exp(-gate))) * up).astype(BF16)
        part = jnp.dot(hid, wbf_ref[2, cols, :], preferred_element_type=F32)
        y = part if y is None else y + part
        issue_piece()
    _store_token_rows(y_ref.at[0, 0], y)

    @pl.when(t == steps - 1)
    def _():
        wait_all(nslot)


def _ffn_call(idx_flat, xn3, wg, wu, wd, layer):
    b = xn3.shape[0] // (SEQ * TOKEN_SUB)
    wspec = pl.BlockSpec((1, 1, D_MODEL, D_MODEL), lambda e, bi, idx: (layer, e, 0, 0))
    return pl.pallas_call(
        _ffn_kernel,
        grid_spec=pltpu.PrefetchScalarGridSpec(
            num_scalar_prefetch=1,
            grid=(N_EXPERTS, b),
            in_specs=[pl.BlockSpec(memory_space=pl.ANY), wspec, wspec, wspec],
            out_specs=pl.BlockSpec((1, 1, CAP * TOKEN_SUB, LANES), lambda e, bi, idx: (bi, e, 0, 0)),
            scratch_shapes=[pltpu.VMEM((2, CAP * TOKEN_SUB, LANES), F32),
                            pltpu.VMEM((3, D_MODEL, D_MODEL), BF16),
                            pltpu.SemaphoreType.DMA((2,))],
        ),
        out_shape=jax.ShapeDtypeStruct((b, N_EXPERTS, CAP * TOKEN_SUB, LANES), F32),
        compiler_params=_cparams(("arbitrary", "arbitrary")),
        name="expert_ffn",
    )(idx_flat, xn3, wg, wu, wd)


def _combine_kernel(row_ref, gate_ref, y_ref, h_hbm, o_hbm, acc_ref, sem_ref):
    b = pl.program_id(0)
    e = pl.program_id(1)
    nb = pl.num_programs(0)
    slot = b % 2
    other = 1 - slot
    load = lambda bb, s: pltpu.make_async_copy(h_hbm.at[bb], acc_ref.at[s], sem_ref.at[0, s])
    drain = lambda bb, s: pltpu.make_async_copy(acc_ref.at[s], o_hbm.at[bb], sem_ref.at[1, s])

    @pl.when((b == 0) & (e == 0))
    def _():
        load(b, slot).start()

    @pl.when(e == 0)
    def _():
        load(b, slot).wait()

    @pl.when((e == 1) & (b >= 1))
    def _():
        drain(b - 1, other).wait()

    @pl.when((e == 1) & (b + 1 < nb))
    def _():
        load(b + 1, other).start()

    base = (b * N_EXPERTS + e) * CAP
    group = 8

    def scatter_add(acc):
        def body(cg, carry):
            first = base + cg * group
            src0 = pl.multiple_of(cg * (group * TOKEN_SUB), group * TOKEN_SUB)
            new = []
            for k in range(group):
                dst = pl.multiple_of(row_ref[first + k], TOKEN_SUB)
                new.append((dst, acc[pl.ds(dst, TOKEN_SUB), :]
                            + y_ref[0, 0, pl.ds(src0 + k * TOKEN_SUB, TOKEN_SUB), :] * gate_ref[first + k]))
            for dst, val in new:
                acc[pl.ds(dst, TOKEN_SUB), :] = val
            return carry

        lax.fori_loop(0, CAP // group, body, 0)

    for s in range(2):
        pl.when(slot == s)(functools.partial(scatter_add, acc_ref.at[s]))

    @pl.when(e == N_EXPERTS - 1)
    def _():
        drain(b, slot).start()

    @pl.when((e == N_EXPERTS - 1) & (b == nb - 1))
    def _():
        drain(b, slot).wait()


def _combine_call(idx_flat, gate_flat, y, h3):
    b = h3.shape[0]
    return pl.pallas_call(
        _combine_kernel,
        grid_spec=pltpu.PrefetchScalarGridSpec(
            num_scalar_prefetch=2,
            grid=(b, N_EXPERTS),
            in_specs=[pl.BlockSpec((1, 1, CAP * TOKEN_SUB, LANES), lambda bi, e, idx, gt: (bi, e, 0, 0)),
                      pl.BlockSpec(memory_space=pl.ANY)],
            out_specs=pl.BlockSpec(memory_space=pl.ANY),
            scratch_shapes=[pltpu.VMEM((2, SEQ * TOKEN_SUB, LANES), F32), pltpu.SemaphoreType.DMA((2, 2))],
        ),
        out_shape=jax.ShapeDtypeStruct(h3.shape, F32),
        compiler_params=_cparams(("arbitrary", "arbitrary")),
        name="expert_combine",
    )(idx_flat, gate_flat, y, h3)


def _norm_kernel(x_ref, g_ref, o_ref):
    o_ref[...] = _rms(_load_token_rows(x_ref, ROW_TILE), g_ref[...])


def _norm_call(x3, gain):
    n = x3.shape[0] // TOKEN_SUB
    tm = ROW_TILE
    return pl.pallas_call(
        _norm_kernel,
        grid=(n // tm,),
        in_specs=[pl.BlockSpec((tm * TOKEN_SUB, LANES), lambda i: (i, 0)),
                  pl.BlockSpec((1, D_MODEL), lambda i: (0, 0))],
        out_specs=pl.BlockSpec((tm, D_MODEL), lambda i: (i, 0)),
        out_shape=jax.ShapeDtypeStruct((n, D_MODEL), F32),
        compiler_params=_cparams(("arbitrary",)),
        name="final_norm",
    )(x3, gain)


def _rope_table(dim, lead):
    half = dim // 2
    inv = 1.0 / (ROPE_THETA ** (jnp.arange(0, dim, 2, dtype=F32) / dim))
    ang = jnp.arange(SEQ, dtype=F32)[:, None] * inv[None, :]
    cos, sin = jnp.cos(ang), jnp.sin(ang)
    zero = jnp.zeros_like(sin)
    if lead:
        tail = jnp.zeros((SEQ, LANES - lead - dim), F32)
        ones = jnp.ones((SEQ, lead), F32)
        zl = jnp.zeros((SEQ, lead), F32)
        c = jnp.concatenate([ones, cos, cos, tail], axis=1)
        sa = jnp.concatenate([zl, zero, sin, tail], axis=1)
        sb = jnp.concatenate([zl, -sin, zero, tail], axis=1)
    else:
        reps = LANES // dim
        c = jnp.concatenate([cos, cos] * reps, axis=1)
        sa = jnp.concatenate([zero, sin] * reps, axis=1)
        sb = jnp.concatenate([-sin, zero] * reps, axis=1)
    return jnp.concatenate([c, sa, sb], axis=1)


def _swa_head_perm():
    return [kvh * SWA_GROUP + g for g in range(SWA_GROUP) for kvh in range(SWA_KV_HEADS)]


def _permute_w_in(w):
    offs = np.cumsum([0, A_W, A_W, A_W, MLA_Q_LORA, MLA_KV_LORA, MLA_ROPE, C_W, 2 * HEAD_DIM, 2 * HEAD_DIM])
    a_q, a_k, a_v, b_cq, b_ckv, b_kr, c_q, c_k, c_v = [w[:, offs[i]:offs[i + 1]] for i in range(9)]
    c_q = jnp.concatenate([c_q[:, h * HEAD_DIM:(h + 1) * HEAD_DIM] for h in _swa_head_perm()], axis=1)
    zeros = lambda n: jnp.zeros((w.shape[0], n), w.dtype)
    kr = jnp.concatenate([zeros(MLA_NOPE), b_kr, zeros(LANES - MLA_NOPE - MLA_ROPE)], axis=1)
    return jnp.concatenate([a_q, a_k, a_v, b_cq, b_ckv, c_q, c_k, c_v, kr], axis=1).astype(BF16)


def _permute_mla(w_uq, w_ukv):
    zq = jnp.zeros((MLA_Q_LORA, LANES - MLA_NOPE - MLA_ROPE), w_uq.dtype)
    zk = jnp.zeros((MLA_KV_LORA, LANES - MLA_NOPE), w_ukv.dtype)
    dq = MLA_NOPE + MLA_ROPE
    dkv = MLA_NOPE + MLA_V
    wq = jnp.concatenate([jnp.concatenate([w_uq[:, h * dq:(h + 1) * dq], zq], axis=1)
                          for h in range(MLA_HEADS)], axis=1)
    wk = jnp.concatenate([jnp.concatenate([w_ukv[:, h * dkv:h * dkv + MLA_NOPE], zk], axis=1)
                          for h in range(MLA_HEADS)], axis=1)
    wv = jnp.concatenate([w_ukv[:, h * dkv + MLA_NOPE:(h + 1) * dkv] for h in range(MLA_HEADS)], axis=1)
    return wq.astype(BF16), jnp.concatenate([wk, wv], axis=1).astype(BF16)


def kernel(x, attn_norm, w_in, na_rpb, mla_q_norm, mla_w_uq, mla_kv_norm, mla_w_ukv, swa_sink, group_norm,
           w_out, ffn_norm, w_router, w_gate, w_up, w_down, final_norm):
    bsz, seq, d = x.shape
    assert (seq, d) == (SEQ, D_MODEL)
    n = bsz * seq
    depth = w_in.shape[0]
    rope_s = _rope_table(HEAD_DIM, 0)
    rope_m = _rope_table(MLA_ROPE, MLA_NOPE)
    c_perm = np.concatenate([np.arange(h * HEAD_DIM, (h + 1) * HEAD_DIM) for h in _swa_head_perm()])

    h2 = x.reshape(n, d)
    for l in range(depth):
        tiled = l > 0
        wq, wkv = _permute_mla(mla_w_uq[l], mla_w_ukv[l])
        aq, ak, av, mq, mk, mv, sq, sk, sv = _proj_call(
            h2, attn_norm[l][None], _permute_w_in(w_in[l]), mla_q_norm[l][None], wq, mla_kv_norm[l][None], wkv,
            rope_s, rope_m, tiled)
        r3 = lambda a: a.reshape(bsz, seq, a.shape[-1])
        o_at = _na_call(r3(aq), r3(ak), av, _na_bias_tables(na_rpb[l]))
        o_bt = _mla_call(r3(mq), r3(mk), mv)
        o_ct = _swa_call(swa_sink[l], r3(sq), r3(sk), sv)

        gn = group_norm[l]
        wo = w_out[l]
        gc = gn[A_W + B_W:][c_perm]
        wc = wo[A_W + B_W:][c_perm]
        hm, xn3, aff_t = _out_call(
            o_at, o_bt, o_ct, h2,
            gn[:A_W, None], gn[A_W:A_W + B_W, None], gc[:, None],
            wo[:A_W].astype(BF16), wo[A_W:A_W + B_W].astype(BF16), wc.astype(BF16),
            ffn_norm[l][None], w_router[l].T, tiled)

        idx, gate = _topk_call(aff_t.reshape(bsz * N_EXPERTS, seq))
        idx_flat = idx.reshape(-1)
        y = _ffn_call(idx_flat, xn3, w_gate, w_up, w_down, l)
        h2 = _combine_call(idx_flat, gate.reshape(-1), y,
                           hm.reshape(bsz, seq * TOKEN_SUB, LANES)).reshape(n * TOKEN_SUB, LANES)
    return _norm_call(h2, final_norm[None]).reshape(bsz, seq, d)
```

```python
import functools

import jax
import jax.numpy as jnp
import numpy as np
from jax import lax
from jax.experimental import pallas as pl
from jax.experimental.pallas import tpu as pltpu

F32 = jnp.float32
BF16 = jnp.bfloat16

D_MODEL = 1024
SEQ = 4096
HEAD_DIM = 64
GRID_W = 64
NA_HEADS = 4
NA_WIN_H = 8
NA_WIN_W = 16
MLA_HEADS = 6
MLA_Q_LORA = 256
MLA_KV_LORA = 128
MLA_NOPE = 64
MLA_ROPE = 32
MLA_V = 64
SWA_HEADS = 6
SWA_KV_HEADS = 2
SWA_GROUP = SWA_HEADS // SWA_KV_HEADS
SWA_WINDOW = 128
ROPE_THETA = 10000.0
N_EXPERTS = 16
EC_CAPACITY = 2
CAP = EC_CAPACITY * SEQ // N_EXPERTS
RMS_EPS = 1e-6
NEG_INF = -1e30

A_W = NA_HEADS * HEAD_DIM
B_W = MLA_HEADS * MLA_V
C_W = SWA_HEADS * HEAD_DIM

LANES = 128
TOKEN_SUB = D_MODEL // LANES
ROW_TILE = 512
VMEM_LIMIT = 56 * 1024 * 1024

P_AQ, P_AK, P_AV = 0, 256, 512
P_CQ, P_CKV = 768, 1024
P_SQ, P_SK, P_SV = 1152, 1536, 1664
P_KR = 1792
P_COLS = 1920

NA_QROWS = 4
NA_KROWS = 12
NA_TQ = NA_QROWS * GRID_W
NA_TK = NA_KROWS * GRID_W
NA_STEP_BLOCKS = 2
MLA_TQ = 512
MLA_TK = 256
VT_PAD = 16
MLA_VT_ROWS = MLA_V + VT_PAD
VT_ROWS = HEAD_DIM + VT_PAD
MLA_AHEAD = 3
WIN_AHEAD = 3
LOG2_E = 1.4426950408889634
FFN_CHUNKS = 4
SWA_TQ = 1024
SWA_SUB = 256
SWA_TK = SWA_SUB + 2 * SWA_WINDOW


def _cparams(sem):
    return pltpu.CompilerParams(dimension_semantics=sem, vmem_limit_bytes=VMEM_LIMIT)


def _rms(x, gain):
    ms = jnp.mean(x * x, axis=-1, keepdims=True)
    return x * lax.rsqrt(ms + RMS_EPS) * gain


def _load_token_rows(ref, rows):
    return jnp.concatenate([ref[pl.ds(s, rows, stride=TOKEN_SUB), :] for s in range(TOKEN_SUB)], axis=1)


def _store_token_rows(ref, val):
    rows = val.shape[0]
    for s in range(TOKEN_SUB):
        ref[pl.ds(s, rows, stride=TOKEN_SUB), :] = val[:, s * LANES:(s + 1) * LANES]


def _nt_dot(a, b):
    return lax.dot_general(a, b, (((1,), (1,)), ((), ())), preferred_element_type=F32)


def _proj_kernel(x_ref, gain_ref, win_ref, qn_ref, wq_ref, kvn_ref, wkv_ref, rs_ref, rm_ref,
                 aq_ref, ak_ref, av_ref, mq_ref, mk_ref, mv_ref, sq_ref, sk_ref, sv_ref, *, tiled):
    x = _load_token_rows(x_ref, ROW_TILE) if tiled else x_ref[...]
    xn = _rms(x, gain_ref[...]).astype(BF16)
    proj = jnp.dot(xn, win_ref[...], preferred_element_type=F32)
    lane = lax.broadcasted_iota(jnp.int32, (1, LANES), 1)
    lo_half = lane < HEAD_DIM

    def rope(grp, tab_ref, half):
        c = tab_ref[:, 0:LANES]
        sa = tab_ref[:, LANES:2 * LANES]
        sb = tab_ref[:, 2 * LANES:3 * LANES]
        return grp * c + pltpu.roll(grp, half, 1) * sa + pltpu.roll(grp, LANES - half, 1) * sb

    def split_heads(grp):
        zero = jnp.zeros_like(grp)
        return jnp.where(lo_half, grp, zero).astype(BF16), jnp.where(lo_half, zero, grp).astype(BF16)

    def store_values_transposed(ref, vals, heads, width):
        vt = vals.T
        ones_row = jnp.where(lax.broadcasted_iota(jnp.int32, (VT_PAD, vt.shape[1]), 0) == 0, 1.0, 0.0)
        for h in range(heads):
            blk = jnp.concatenate([vt[h * width:(h + 1) * width], ones_row], axis=0)
            ref[h * (width + VT_PAD):(h + 1) * (width + VT_PAD), :] = blk.astype(BF16)

    na_scale = HEAD_DIM ** -0.5 * LOG2_E
    for p in range(NA_HEADS // 2):
        grp = proj[:, P_AQ + p * LANES:P_AQ + (p + 1) * LANES] * na_scale
        q0, q1 = split_heads(grp)
        aq_ref[:, (2 * p) * LANES:(2 * p + 1) * LANES] = q0
        aq_ref[:, (2 * p + 1) * LANES:(2 * p + 2) * LANES] = q1
    ak_ref[...] = proj[:, P_AK:P_AK + A_W].astype(BF16)
    store_values_transposed(av_ref, proj[:, P_AV:P_AV + A_W], NA_HEADS, HEAD_DIM)

    cq = _rms(proj[:, P_CQ:P_CQ + MLA_Q_LORA], qn_ref[...]).astype(BF16)
    q = jnp.dot(cq, wq_ref[...], preferred_element_type=F32) * ((MLA_NOPE + MLA_ROPE) ** -0.5 * LOG2_E)
    ckv = _rms(proj[:, P_CKV:P_CKV + MLA_KV_LORA], kvn_ref[...]).astype(BF16)
    kv = jnp.dot(ckv, wkv_ref[...], preferred_element_type=F32)
    kr = rope(proj[:, P_KR:P_KR + LANES], rm_ref, MLA_ROPE // 2)
    for h in range(MLA_HEADS):
        sl = slice(h * LANES, (h + 1) * LANES)
        mq_ref[:, sl] = rope(q[:, sl], rm_ref, MLA_ROPE // 2).astype(BF16)
        mk_ref[:, sl] = (kv[:, sl] + kr).astype(BF16)
    store_values_transposed(mv_ref, kv[:, MLA_HEADS * LANES:MLA_HEADS * LANES + B_W], MLA_HEADS, MLA_V)

    swa_scale = HEAD_DIM ** -0.5 * LOG2_E
    for g in range(SWA_GROUP):
        grp = rope(proj[:, P_SQ + g * LANES:P_SQ + (g + 1) * LANES], rs_ref, HEAD_DIM // 2) * swa_scale
        q0, q1 = split_heads(grp)
        sq_ref[:, (2 * g) * LANES:(2 * g + 1) * LANES] = q0
        sq_ref[:, (2 * g + 1) * LANES:(2 * g + 2) * LANES] = q1
    sk_ref[...] = rope(proj[:, P_SK:P_SK + LANES], rs_ref, HEAD_DIM // 2).astype(BF16)
    store_values_transposed(sv_ref, proj[:, P_SV:P_SV + LANES], SWA_KV_HEADS, HEAD_DIM)


def _proj_call(x2, gain, win, qn, wq, kvn, wkv, rope_s, rope_m, tiled):
    tm = ROW_TILE
    n = x2.shape[0] // TOKEN_SUB if tiled else x2.shape[0]
    xspec = pl.BlockSpec((tm * TOKEN_SUB, LANES) if tiled else (tm, D_MODEL), lambda i: (i, 0))
    seq_tiles = SEQ // tm
    row = lambda w: pl.BlockSpec((tm, w), lambda i: (i, 0))
    full = lambda a: pl.BlockSpec(a.shape, lambda i: (0,) * a.ndim)
    pos = lambda w: pl.BlockSpec((tm, w), lambda i: (i % seq_tiles, 0))
    outs = ((2 * A_W, False), (A_W, False), (NA_HEADS * VT_ROWS, True),
            (MLA_HEADS * LANES, False), (MLA_HEADS * LANES, False), (MLA_HEADS * MLA_VT_ROWS, True),
            (2 * C_W, False), (LANES, False), (SWA_KV_HEADS * VT_ROWS, True))
    out_specs = [pl.BlockSpec((w, tm), lambda i: (0, i)) if t else row(w) for w, t in outs]
    out_shape = [jax.ShapeDtypeStruct((w, n) if t else (n, w), BF16) for w, t in outs]
    return pl.pallas_call(
        functools.partial(_proj_kernel, tiled=tiled),
        grid=(n // tm,),
        in_specs=[xspec, full(gain), full(win), full(qn), full(wq), full(kvn), full(wkv),
                  pos(3 * LANES), pos(3 * LANES)],
        out_specs=out_specs,
        out_shape=out_shape,
        compiler_params=_cparams(("arbitrary",)),
        name="norm_in_proj",
    )(x2, gain, win, qn, wq, kvn, wkv, rope_s, rope_m)


def _na_key_start(j):
    rows = SEQ // GRID_W
    return jnp.clip(NA_QROWS * j - NA_WIN_H // 2, 0, rows - NA_KROWS)


def _na_kernel(q_ref, k_ref, vt_ref, bias0_ref, bias1_ref, o_ref):
    bias_refs = (bias0_ref, bias1_ref)
    starts, kwins = [], []
    for jj in range(NA_STEP_BLOCKS):
        ks = pl.multiple_of(_na_key_start(NA_STEP_BLOCKS * pl.program_id(1) + jj) * GRID_W, 256)
        starts.append(ks)
        kwins.append(k_ref[0, pl.ds(ks, NA_TK), :])

    def scores(item):
        jj, h = divmod(item, NA_HEADS)
        p = h // 2
        q = q_ref[0, jj * NA_TQ:(jj + 1) * NA_TQ, h * LANES:(h + 1) * LANES]
        return _nt_dot(kwins[jj][:, p * LANES:(p + 1) * LANES], q) + bias_refs[jj][0, h]

    items = NA_STEP_BLOCKS * NA_HEADS
    ahead = [scores(it) for it in range(WIN_AHEAD)]
    for item in range(items):
        jj, h = divmod(item, NA_HEADS)
        st = ahead.pop(0)
        if item + WIN_AHEAD < items:
            ahead.append(scores(item + WIN_AHEAD))
        m = jnp.max(st, axis=0, keepdims=True)
        e = jnp.exp2(st - m).astype(BF16)
        pv = jnp.dot(vt_ref[h * VT_ROWS:(h + 1) * VT_ROWS, pl.ds(starts[jj], NA_TK)], e,
                     preferred_element_type=F32)
        out = pv[:HEAD_DIM] / pv[HEAD_DIM:HEAD_DIM + 1]
        o_ref[h * HEAD_DIM:(h + 1) * HEAD_DIM, jj * NA_TQ:(jj + 1) * NA_TQ] = out.astype(o_ref.dtype)


def _na_call(aq, ak, avt, bias_t):
    b = aq.shape[0]
    nj = SEQ // NA_TQ
    steps = nj // NA_STEP_BLOCKS

    def variant(jj):
        def index_map(bi, i):
            j = NA_STEP_BLOCKS * i + jj
            return (jnp.where(j == 0, 0, jnp.where(j == nj - 1, 2, 1)), 0, 0, 0)
        return index_map

    bias_spec = lambda jj: pl.BlockSpec((1, NA_HEADS, NA_TK, NA_TQ), variant(jj))
    tq = NA_STEP_BLOCKS * NA_TQ
    return pl.pallas_call(
        _na_kernel,
        grid=(b, steps),
        in_specs=[pl.BlockSpec((1, tq, 2 * A_W), lambda bi, i: (bi, i, 0)),
                  pl.BlockSpec((1, SEQ, A_W), lambda bi, i: (bi, 0, 0)),
                  pl.BlockSpec((NA_HEADS * VT_ROWS, SEQ), lambda bi, i: (0, bi)),
                  bias_spec(0), bias_spec(1)],
        out_specs=pl.BlockSpec((A_W, tq), lambda bi, i: (0, bi * steps + i)),
        out_shape=jax.ShapeDtypeStruct((A_W, b * SEQ), BF16),
        compiler_params=_cparams(("arbitrary", "arbitrary")),
        name="neighbourhood_attn",
    )(aq, ak, avt, bias_t, bias_t)


def _na_bias_tables(rpb):
    rows = SEQ // GRID_W
    nj = SEQ // NA_TQ
    c = np.arange(GRID_W)[:, None]
    kc = np.arange(GRID_W)[None, :]
    cs = np.clip(c - NA_WIN_W // 2, 0, GRID_W - NA_WIN_W)
    cvalid = (kc >= cs) & (kc < cs + NA_WIN_W)
    pad = GRID_W - NA_WIN_W
    padded = jnp.pad(rpb, ((0, 0), (0, 0), (pad, pad)))
    toep = jnp.stack([padded[:, :, GRID_W - 1 - ci:2 * GRID_W - 1 - ci] for ci in range(GRID_W)], axis=2)
    toep = jnp.where(cvalid[None, None], toep * LOG2_E, NEG_INF)
    toep_t = jnp.swapaxes(toep, 2, 3)
    masked = jnp.full((rpb.shape[0], GRID_W, GRID_W), NEG_INF, F32)
    variants = []
    for j in (0, 1, nj - 1):
        ks = int(np.clip(NA_QROWS * j - NA_WIN_H // 2, 0, rows - NA_KROWS))
        key_blocks = []
        for u in range(NA_KROWS):
            kr = ks + u
            blocks = []
            for i in range(NA_QROWS):
                r = NA_QROWS * j + i
                rs = int(np.clip(r - NA_WIN_H // 2, 0, rows - NA_WIN_H))
                blocks.append(toep_t[:, kr - r + NA_WIN_H - 1] if rs <= kr < rs + NA_WIN_H else masked)
            key_blocks.append(jnp.concatenate(blocks, axis=2))
        variants.append(jnp.concatenate(key_blocks, axis=1))
    return jnp.stack(variants)


def _mla_kernel(q_ref, k_ref, vt_ref, o_ref):
    heads = range(2)
    nchunks = SEQ // MLA_TK

    def scores(c, hh):
        sl = slice(hh * LANES, (hh + 1) * LANES)
        return _nt_dot(k_ref[0, c * MLA_TK:(c + 1) * MLA_TK, sl], q_ref[0, :, sl])

    m = [jnp.full((1, MLA_TQ), NEG_INF, F32) for _ in heads]
    acc = [jnp.zeros((MLA_VT_ROWS, MLA_TQ), F32) for _ in heads]
    items = [(c, hh) for c in range(nchunks) for hh in heads]
    ahead = [scores(*it) for it in items[:MLA_AHEAD]]
    for n, (c, hh) in enumerate(items):
        st = ahead.pop(0)
        if n + MLA_AHEAD < len(items):
            ahead.append(scores(*items[n + MLA_AHEAD]))
        m_new = jnp.maximum(m[hh], jnp.max(st, axis=0, keepdims=True))
        alpha = jnp.exp2(m[hh] - m_new)
        e = jnp.exp2(st - m_new).astype(BF16)
        vt = vt_ref[hh * MLA_VT_ROWS:(hh + 1) * MLA_VT_ROWS, c * MLA_TK:(c + 1) * MLA_TK]
        acc[hh] = alpha * acc[hh] + jnp.dot(vt, e, preferred_element_type=F32)
        m[hh] = m_new
    out = jnp.concatenate([acc[hh][:MLA_V] / acc[hh][MLA_V:MLA_V + 1] for hh in heads], axis=0)
    o_ref[...] = out.astype(o_ref.dtype)


def _mla_call(mq, mk, mvt):
    b = mq.shape[0]
    pairs = MLA_HEADS // 2
    nq = SEQ // MLA_TQ
    return pl.pallas_call(
        _mla_kernel,
        grid=(b, pairs, nq),
        in_specs=[pl.BlockSpec((1, MLA_TQ, 2 * LANES), lambda bi, p, i: (bi, i, p)),
                  pl.BlockSpec((1, SEQ, 2 * LANES), lambda bi, p, i: (bi, 0, p)),
                  pl.BlockSpec((2 * MLA_VT_ROWS, SEQ), lambda bi, p, i: (p, bi))],
        out_specs=pl.BlockSpec((2 * MLA_V, MLA_TQ), lambda bi, p, i: (p, bi * nq + i)),
        out_shape=jax.ShapeDtypeStruct((B_W, b * SEQ), BF16),
        compiler_params=_cparams(("arbitrary", "arbitrary", "arbitrary")),
        name="latent_attn",
    )(mq, mk, mvt)


def _swa_kernel(sink_ref, q_ref, k_ref, vt_ref, o_ref):
    i = pl.program_id(1)
    nsub = SWA_TQ // SWA_SUB
    starts, kwins, valids = [], [], []
    for sb in range(nsub):
        q0 = SWA_TQ * i + sb * SWA_SUB
        ws = pl.multiple_of(jnp.clip(q0 - SWA_WINDOW, 0, SEQ - SWA_TK), LANES)
        kpos = ws + lax.broadcasted_iota(jnp.int32, (SWA_TK, 1), 0)
        qpos = q0 + lax.broadcasted_iota(jnp.int32, (1, SWA_SUB), 1)
        starts.append(ws)
        kwins.append(k_ref[0, pl.ds(ws, SWA_TK), :])
        valids.append(jnp.abs(qpos - kpos) <= SWA_WINDOW)

    def scores(item):
        sb, grp = divmod(item, SWA_HEADS)
        st = _nt_dot(kwins[sb], q_ref[0, sb * SWA_SUB:(sb + 1) * SWA_SUB, grp * LANES:(grp + 1) * LANES])
        return jnp.where(valids[sb], st, NEG_INF)

    items = nsub * SWA_HEADS
    ahead = [scores(it) for it in range(WIN_AHEAD)]
    for item in range(items):
        sb, grp = divmod(item, SWA_HEADS)
        g, kvh = divmod(grp, SWA_KV_HEADS)
        st = ahead.pop(0)
        if item + WIN_AHEAD < items:
            ahead.append(scores(item + WIN_AHEAD))
        sink = sink_ref[kvh * SWA_GROUP + g] * LOG2_E
        m = jnp.maximum(jnp.max(st, axis=0, keepdims=True), sink)
        e = jnp.exp2(st - m).astype(BF16)
        pv = jnp.dot(vt_ref[kvh * VT_ROWS:(kvh + 1) * VT_ROWS, pl.ds(starts[sb], SWA_TK)], e,
                     preferred_element_type=F32)
        l = pv[HEAD_DIM:HEAD_DIM + 1] + jnp.exp2(sink - m)
        out = pv[:HEAD_DIM] / l
        o_ref[grp * HEAD_DIM:(grp + 1) * HEAD_DIM, sb * SWA_SUB:(sb + 1) * SWA_SUB] = out.astype(o_ref.dtype)


def _swa_call(sink, sq, sk, svt):
    b = sq.shape[0]
    ni = SEQ // SWA_TQ
    return pl.pallas_call(
        _swa_kernel,
        grid=(b, ni),
        in_specs=[pl.BlockSpec(memory_space=pltpu.SMEM),
                  pl.BlockSpec((1, SWA_TQ, 2 * C_W), lambda bi, i: (bi, i, 0)),
                  pl.BlockSpec((1, SEQ, LANES), lambda bi, i: (bi, 0, 0)),
                  pl.BlockSpec((SWA_KV_HEADS * VT_ROWS, SEQ), lambda bi, i: (0, bi))],
        out_specs=pl.BlockSpec((C_W, SWA_TQ), lambda bi, i: (0, bi * ni + i)),
        out_shape=jax.ShapeDtypeStruct((C_W, b * SEQ), BF16),
        compiler_params=_cparams(("arbitrary", "arbitrary")),
        name="window_gqa",
    )(sink, sq, sk, svt)


def _out_kernel(oa_ref, ob_ref, oc_ref, h_ref, ga_ref, gb_ref, gc_ref, wa_ref, wb_ref, wc_ref,
                gf_ref, wr_ref, hm_ref, xn_ref, aff_ref, *, tiled):
    acc = _load_token_rows(h_ref, ROW_TILE) if tiled else h_ref[...]
    for o_ref, g_ref, w_ref in ((oa_ref, ga_ref, wa_ref), (ob_ref, gb_ref, wb_ref), (oc_ref, gc_ref, wc_ref)):
        o = o_ref[...].astype(F32)
        ms = jnp.mean(o * o, axis=0, keepdims=True)
        on = (o * lax.rsqrt(ms + RMS_EPS) * g_ref[...]).astype(BF16)
        acc = acc + lax.dot_general(on, w_ref[...], (((0,), (0,)), ((), ())), preferred_element_type=F32)
    _store_token_rows(hm_ref, acc)
    xn = _rms(acc, gf_ref[...])
    _store_token_rows(xn_ref, xn)
    wr = wr_ref[...]
    wr_hi = wr.astype(BF16)
    wr_lo = (wr - wr_hi.astype(F32)).astype(BF16)
    xn_hi = xn.astype(BF16)
    xn_lo = (xn - xn_hi.astype(F32)).astype(BF16)
    logits = _nt_dot(wr_hi, xn_hi) + (_nt_dot(wr_hi, xn_lo) + _nt_dot(wr_lo, xn_hi))
    m = jnp.max(logits, axis=0, keepdims=True)
    e = jnp.exp(logits - m)
    aff_ref[0] = e / jnp.sum(e, axis=0, keepdims=True)


def _out_call(oa, ob, oc, h2, ga, gb, gc, wa, wb, wc, gf, wr_t, tiled):
    n = oa.shape[1]
    tm = ROW_TILE
    col = lambda w: pl.BlockSpec((w, tm), lambda i: (0, i))
    tok = pl.BlockSpec((tm * TOKEN_SUB, LANES), lambda i: (i, 0))
    hspec = tok if tiled else pl.BlockSpec((tm, D_MODEL), lambda i: (i, 0))
    seq_tiles = SEQ // tm
    row = lambda w: pl.BlockSpec((tm, w), lambda i: (i, 0))
    full = lambda a: pl.BlockSpec(a.shape, lambda i: (0,) * a.ndim)
    return pl.pallas_call(
        functools.partial(_out_kernel, tiled=tiled),
        grid=(n // tm,),
        in_specs=[col(A_W), col(B_W), col(C_W), hspec, full(ga), full(gb), full(gc),
                  full(wa), full(wb), full(wc), full(gf), full(wr_t)],
        out_specs=[tok, tok,
                   pl.BlockSpec((1, N_EXPERTS, tm), lambda i: (i // seq_tiles, 0, i % seq_tiles))],
        out_shape=[jax.ShapeDtypeStruct((n * TOKEN_SUB, LANES), F32), jax.ShapeDtypeStruct((n * TOKEN_SUB, LANES), F32),
                   jax.ShapeDtypeStruct((n // SEQ, N_EXPERTS, SEQ), F32)],
        compiler_params=_cparams(("arbitrary",)),
        name="out_proj_router",
    )(oa, ob, oc, h2, ga, gb, gc, wa, wb, wc, gf, wr_t)


def _topk_kernel(aff_ref, idx_ref, gate_ref, posl_ref, affr_ref):
    rows, seq = aff_ref.shape
    tiles = seq // LANES
    a = aff_ref[...]
    int_min = jnp.int32(-2 ** 31)

    def ordered_to_float(u):
        key = u ^ int_min
        bits = key ^ (lax.shift_right_arithmetic(key, jnp.int32(31)) & jnp.int32(0x7FFFFFFF))
        return lax.bitcast_convert_type(bits, F32)

    t_u = jnp.zeros((rows, 1), jnp.int32)
    for bit in range(31, -1, -1):
        step = int_min if bit == 31 else jnp.int32(1 << bit)
        cand_u = t_u | step
        cnt = jnp.sum(jnp.where(a >= ordered_to_float(cand_u), 1.0, 0.0), axis=1, keepdims=True)
        t_u = jnp.where(cnt >= CAP, cand_u, t_u)
    thr = ordered_to_float(t_u)
    gt = a > thr
    eq = a == thr
    need = CAP - jnp.sum(jnp.where(gt, 1.0, 0.0), axis=1, keepdims=True)

    tri = jnp.where(lax.broadcasted_iota(jnp.int32, (LANES, LANES), 0)
                    <= lax.broadcasted_iota(jnp.int32, (LANES, LANES), 1), 1.0, 0.0).astype(BF16)

    def prefix_incl(flags_f32, t, carry):
        blk = flags_f32[:, t * LANES:(t + 1) * LANES]
        inc = jnp.dot(blk.astype(BF16), tri, preferred_element_type=F32) + carry
        return blk, inc, inc[:, LANES - 1:LANES]

    eq_f = jnp.where(eq, 1.0, 0.0)
    gt_f = jnp.where(gt, 1.0, 0.0)
    carry_eq = jnp.zeros((rows, 1), F32)
    for t in range(tiles):
        eq_blk, eq_inc, carry_eq = prefix_incl(eq_f, t, carry_eq)
        sel_blk = jnp.maximum(gt_f[:, t * LANES:(t + 1) * LANES],
                              jnp.where(eq_inc <= need, eq_blk, 0.0))
        sel_loc = jnp.dot(sel_blk.astype(BF16), tri, preferred_element_type=F32)
        posl_ref[t * rows:(t + 1) * rows, :] = jnp.where(sel_blk > 0.0, sel_loc - 1.0, -1.0)
        affr_ref[t * rows:(t + 1) * rows, :] = a[:, t * LANES:(t + 1) * LANES]

    slot = lax.broadcasted_iota(jnp.int32, (1, CAP), 1).astype(F32)
    tile_id = lax.broadcasted_iota(jnp.int32, (tiles, 1), 0).astype(F32)
    lane_id = lax.broadcasted_iota(jnp.int32, (LANES, 1), 0).astype(F32)
    before = jnp.where(lax.broadcasted_iota(jnp.int32, (tiles, tiles), 1)
                       < lax.broadcasted_iota(jnp.int32, (tiles, tiles), 0), 1.0, 0.0).astype(BF16)
    sub8 = lax.broadcasted_iota(jnp.int32, (8, CAP), 0)
    tn_dot = lambda x, onehot: lax.dot_general(x.astype(BF16), onehot, (((0,), (0,)), ((), ())),
                                               preferred_element_type=F32)

    def row_group(i8, carry):
        r8 = pl.multiple_of(i8 * 8, 8)
        idx8 = jnp.zeros((8, CAP), F32)
        gate8 = jnp.zeros((8, CAP), F32)
        for k in range(8):
            i = r8 + k
            ranks = posl_ref[pl.ds(i, tiles, stride=rows), :]
            affs = affr_ref[pl.ds(i, tiles, stride=rows), :]
            count = jnp.max(ranks, axis=1, keepdims=True) + 1.0
            start = jnp.dot(before, jnp.broadcast_to(count, (tiles, LANES)).astype(BF16),
                            preferred_element_type=F32)[:, 0:1]
            tile_of = jnp.sum(jnp.where(start <= slot, 1.0, 0.0), axis=0, keepdims=True) - 1.0
            pick = tile_id == tile_of
            onehot = jnp.where(pick, 1.0, 0.0).astype(BF16)
            rank = slot - jnp.sum(jnp.where(pick, start, 0.0), axis=0, keepdims=True)
            match = tn_dot(ranks, onehot) == rank
            g0 = affs.astype(BF16).astype(F32)
            r1 = affs - g0
            g1 = r1.astype(BF16).astype(F32)
            vals = tn_dot(g0, onehot) + tn_dot(g1, onehot) + tn_dot(r1 - g1, onehot)
            tok = tile_of * LANES + jnp.sum(jnp.where(match, lane_id, 0.0), axis=0, keepdims=True)
            tok = tok * TOKEN_SUB
            gate = jnp.sum(jnp.where(match, vals, 0.0), axis=0, keepdims=True)
            idx8 = jnp.where(sub8 == k, jnp.broadcast_to(tok, (8, CAP)), idx8)
            gate8 = jnp.where(sub8 == k, jnp.broadcast_to(gate, (8, CAP)), gate8)
        idx_ref[pl.ds(r8, 8), :] = idx8.astype(jnp.int32)
        gate_ref[pl.ds(r8, 8), :] = gate8
        return carry

    lax.fori_loop(0, rows // 8, row_group, 0)


def _topk_call(aff2):
    rows, seq = aff2.shape
    return pl.pallas_call(
        _topk_kernel,
        out_shape=[jax.ShapeDtypeStruct((rows, CAP), jnp.int32), jax.ShapeDtypeStruct((rows, CAP), F32)],
        scratch_shapes=[pltpu.VMEM((seq // LANES * rows, LANES), F32), pltpu.VMEM((seq // LANES * rows, LANES), F32)],
        compiler_params=pltpu.CompilerParams(vmem_limit_bytes=VMEM_LIMIT),
        name="expert_choice_topk",
    )(aff2)


def _ffn_kernel(row_ref, x_hbm, wg_ref, wu_ref, wd_ref, y_ref, xs_ref, wbf_ref, sem_ref):
    e = pl.program_id(0)
    b = pl.program_id(1)
    nb = pl.num_programs(1)
    steps = N_EXPERTS * nb
    t = e * nb + b

    def issue(step, slot):
        sb = step % nb
        base = (sb * N_EXPERTS + step // nb) * CAP
        row0 = sb * (SEQ * TOKEN_SUB)

        def body(c, carry):
            src = pl.multiple_of(row0 + row_ref[base + c], TOKEN_SUB)
            dst = pl.multiple_of(c * TOKEN_SUB, TOKEN_SUB)
            pltpu.make_async_copy(x_hbm.at[pl.ds(src, TOKEN_SUB), :], xs_ref.at[slot, pl.ds(dst, TOKEN_SUB), :],
                                  sem_ref.at[slot]).start()
            return carry

        lax.fori_loop(0, CAP, body, 0, unroll=8)

    @pl.when(t == 0)
    def _():
        issue(t, 0)

    @pl.when(b == 0)
    def _():
        wbf_ref[0] = wg_ref[0, 0].astype(BF16)
        wbf_ref[1] = wu_ref[0, 0].astype(BF16)
        wbf_ref[2] = wd_ref[0, 0].astype(BF16)

    slot = t % 2
    wait_all = lambda s: pltpu.make_async_copy(x_hbm.at[pl.ds(0, CAP * TOKEN_SUB), :], xs_ref.at[s],
                                               sem_ref.at[s]).wait()
    wait_all(slot)
    xs = _load_token_rows(xs_ref.at[slot], CAP).astype(BF16)

    nstep = jnp.where(t + 1 < steps, t + 1, 0)
    nslot = 1 - slot
    nsb = nstep % nb
    nbase = (nsb * N_EXPERTS + nstep // nb) * CAP
    nrow0 = nsb * (SEQ * TOKEN_SUB)
    pieces = 3 * FFN_CHUNKS
    bounds = [CAP * k // pieces for k in range(pieces + 1)]
    piece = iter(range(pieces))

    def issue_piece():
        k = next(piece)
        for c in range(bounds[k], bounds[k + 1]):
            src = pl.multiple_of(nrow0 + row_ref[nbase + c], TOKEN_SUB)
            pltpu.make_async_copy(x_hbm.at[pl.ds(src, TOKEN_SUB), :],
                                  xs_ref.at[nslot, pl.ds(c * TOKEN_SUB, TOKEN_SUB), :], sem_ref.at[nslot]).start()

    width = D_MODEL // FFN_CHUNKS
    y = None
    for j in range(FFN_CHUNKS):
        cols = slice(j * width, (j + 1) * width)
        gate = jnp.dot(xs, wbf_ref[0, :, cols], preferred_element_type=F32)
        issue_piece()
        up = jnp.dot(xs, wbf_ref[1, :, cols], preferred_element_type=F32)
        issue_piece()
        hid = (gate * (1.0 / (1.0 + jnp.exp(-gate))) * up).astype(BF16)
        part = jnp.dot(hid, wbf_ref[2, cols, :], preferred_element_type=F32)
        y = part if y is None else y + part
        issue_piece()
    _store_token_rows(y_ref.at[0, 0], y)

    @pl.when(t == steps - 1)
    def _():
        wait_all(nslot)


def _ffn_call(idx_flat, xn3, wg, wu, wd, layer):
    b = xn3.shape[0] // (SEQ * TOKEN_SUB)
    wspec = pl.BlockSpec((1, 1, D_MODEL, D_MODEL), lambda e, bi, idx: (layer, e, 0, 0))
    return pl.pallas_call(
        _ffn_kernel,
        grid_spec=pltpu.PrefetchScalarGridSpec(
            num_scalar_prefetch=1,
            grid=(N_EXPERTS, b),
            in_specs=[pl.BlockSpec(memory_space=pl.ANY), wspec, wspec, wspec],
            out_specs=pl.BlockSpec((1, 1, CAP * TOKEN_SUB, LANES), lambda e, bi, idx: (bi, e, 0, 0)),
            scratch_shapes=[pltpu.VMEM((2, CAP * TOKEN_SUB, LANES), F32),
                            pltpu.VMEM((3, D_MODEL, D_MODEL), BF16),
                            pltpu.SemaphoreType.DMA((2,))],
        ),
        out_shape=jax.ShapeDtypeStruct((b, N_EXPERTS, CAP * TOKEN_SUB, LANES), F32),
        compiler_params=_cparams(("arbitrary", "arbitrary")),
        name="expert_ffn",
    )(idx_flat, xn3, wg, wu, wd)


def _combine_kernel(row_ref, gate_ref, y_ref, h_hbm, o_hbm, acc_ref, sem_ref):
    b = pl.program_id(0)
    e = pl.program_id(1)
    nb = pl.num_programs(0)
    slot = b % 2
    other = 1 - slot
    load = lambda bb, s: pltpu.make_async_copy(h_hbm.at[bb], acc_ref.at[s], sem_ref.at[0, s])
    drain = lambda bb, s: pltpu.make_async_copy(acc_ref.at[s], o_hbm.at[bb], sem_ref.at[1, s])

    @pl.when((b == 0) & (e == 0))
    def _():
        load(b, slot).start()

    @pl.when(e == 0)
    def _():
        load(b, slot).wait()

    @pl.when((e == 1) & (b >= 1))
    def _():
        drain(b - 1, other).wait()

    @pl.when((e == 1) & (b + 1 < nb))
    def _():
        load(b + 1, other).start()

    base = (b * N_EXPERTS + e) * CAP
    group = 8

    def scatter_add(acc):
        def body(cg, carry):
            first = base + cg * group
            src0 = pl.multiple_of(cg * (group * TOKEN_SUB), group * TOKEN_SUB)
            new = []
            for k in range(group):
                dst = pl.multiple_of(row_ref[first + k], TOKEN_SUB)
                new.append((dst, acc[pl.ds(dst, TOKEN_SUB), :]
                            + y_ref[0, 0, pl.ds(src0 + k * TOKEN_SUB, TOKEN_SUB), :] * gate_ref[first + k]))
            for dst, val in new:
                acc[pl.ds(dst, TOKEN_SUB), :] = val
            return carry

        lax.fori_loop(0, CAP // group, body, 0)

    for s in range(2):
        pl.when(slot == s)(functools.partial(scatter_add, acc_ref.at[s]))

    @pl.when(e == N_EXPERTS - 1)
    def _():
        drain(b, slot).start()

    @pl.when((e == N_EXPERTS - 1) & (b == nb - 1))
    def _():
        drain(b, slot).wait()


def _combine_call(idx_flat, gate_flat, y, h3):
    b = h3.shape[0]
    return pl.pallas_call(
        _combine_kernel,
        grid_spec=pltpu.PrefetchScalarGridSpec(
            num_scalar_prefetch=2,
            grid=(b, N_EXPERTS),
            in_specs=[pl.BlockSpec((1, 1, CAP * TOKEN_SUB, LANES), lambda bi, e, idx, gt: (bi, e, 0, 0)),
                      pl.BlockSpec(memory_space=pl.ANY)],
            out_specs=pl.BlockSpec(memory_space=pl.ANY),
            scratch_shapes=[pltpu.VMEM((2, SEQ * TOKEN_SUB, LANES), F32), pltpu.SemaphoreType.DMA((2, 2))],
        ),
        out_shape=jax.ShapeDtypeStruct(h3.shape, F32),
        compiler_params=_cparams(("arbitrary", "arbitrary")),
        name="expert_combine",
    )(idx_flat, gate_flat, y, h3)


def _norm_kernel(x_ref, g_ref, o_ref):
    o_ref[...] = _rms(_load_token_rows(x_ref, ROW_TILE), g_ref[...])


def _norm_call(x3, gain):
    n = x3.shape[0] // TOKEN_SUB
    tm = ROW_TILE
    return pl.pallas_call(
        _norm_kernel,
        grid=(n // tm,),
        in_specs=[pl.BlockSpec((tm * TOKEN_SUB, LANES), lambda i: (i, 0)),
                  pl.BlockSpec((1, D_MODEL), lambda i: (0, 0))],
        out_specs=pl.BlockSpec((tm, D_MODEL), lambda i: (i, 0)),
        out_shape=jax.ShapeDtypeStruct((n, D_MODEL), F32),
        compiler_params=_cparams(("arbitrary",)),
        name="final_norm",
    )(x3, gain)


def _rope_table(dim, lead):
    half = dim // 2
    inv = 1.0 / (ROPE_THETA ** (jnp.arange(0, dim, 2, dtype=F32) / dim))
    ang = jnp.arange(SEQ, dtype=F32)[:, None] * inv[None, :]
    cos, sin = jnp.cos(ang), jnp.sin(ang)
    zero = jnp.zeros_like(sin)
    if lead:
        tail = jnp.zeros((SEQ, LANES - lead - dim), F32)
        ones = jnp.ones((SEQ, lead), F32)
        zl = jnp.zeros((SEQ, lead), F32)
        c = jnp.concatenate([ones, cos, cos, tail], axis=1)
        sa = jnp.concatenate([zl, zero, sin, tail], axis=1)
        sb = jnp.concatenate([zl, -sin, zero, tail], axis=1)
    else:
        reps = LANES // dim
        c = jnp.concatenate([cos, cos] * reps, axis=1)
        sa = jnp.concatenate([zero, sin] * reps, axis=1)
        sb = jnp.concatenate([-sin, zero] * reps, axis=1)
    return jnp.concatenate([c, sa, sb], axis=1)


def _swa_head_perm():
    return [kvh * SWA_GROUP + g for g in range(SWA_GROUP) for kvh in range(SWA_KV_HEADS)]


def _permute_w_in(w):
    offs = np.cumsum([0, A_W, A_W, A_W, MLA_Q_LORA, MLA_KV_LORA, MLA_ROPE, C_W, 2 * HEAD_DIM, 2 * HEAD_DIM])
    a_q, a_k, a_v, b_cq, b_ckv, b_kr, c_q, c_k, c_v = [w[:, offs[i]:offs[i + 1]] for i in range(9)]
    c_q = jnp.concatenate([c_q[:, h * HEAD_DIM:(h + 1) * HEAD_DIM] for h in _swa_head_perm()], axis=1)
    zeros = lambda n: jnp.zeros((w.shape[0], n), w.dtype)
    kr = jnp.concatenate([zeros(MLA_NOPE), b_kr, zeros(LANES - MLA_NOPE - MLA_ROPE)], axis=1)
    return jnp.concatenate([a_q, a_k, a_v, b_cq, b_ckv, c_q, c_k, c_v, kr], axis=1).astype(BF16)


def _permute_mla(w_uq, w_ukv):
    zq = jnp.zeros((MLA_Q_LORA, LANES - MLA_NOPE - MLA_ROPE), w_uq.dtype)
    zk = jnp.zeros((MLA_KV_LORA, LANES - MLA_NOPE), w_ukv.dtype)
    dq = MLA_NOPE + MLA_ROPE
    dkv = MLA_NOPE + MLA_V
    wq = jnp.concatenate([jnp.concatenate([w_uq[:, h * dq:(h + 1) * dq], zq], axis=1)
                          for h in range(MLA_HEADS)], axis=1)
    wk = jnp.concatenate([jnp.concatenate([w_ukv[:, h * dkv:h * dkv + MLA_NOPE], zk], axis=1)
                          for h in range(MLA_HEADS)], axis=1)
    wv = jnp.concatenate([w_ukv[:, h * dkv + MLA_NOPE:(h + 1) * dkv] for h in range(MLA_HEADS)], axis=1)
    return wq.astype(BF16), jnp.concatenate([wk, wv], axis=1).astype(BF16)


def kernel(x, attn_norm, w_in, na_rpb, mla_q_norm, mla_w_uq, mla_kv_norm, mla_w_ukv, swa_sink, group_norm,
           w_out, ffn_norm, w_router, w_gate, w_up, w_down, final_norm):
    bsz, seq, d = x.shape
    assert (seq, d) == (SEQ, D_MODEL)
    n = bsz * seq
    depth = w_in.shape[0]
    rope_s = _rope_table(HEAD_DIM, 0)
    rope_m = _rope_table(MLA_ROPE, MLA_NOPE)
    c_perm = np.concatenate([np.arange(h * HEAD_DIM, (h + 1) * HEAD_DIM) for h in _swa_head_perm()])

    h2 = x.reshape(n, d)
    for l in range(depth):
        tiled = l > 0
        wq, wkv = _permute_mla(mla_w_uq[l], mla_w_ukv[l])
        aq, ak, av, mq, mk, mv, sq, sk, sv = _proj_call(
            h2, attn_norm[l][None], _permute_w_in(w_in[l]), mla_q_norm[l][None], wq, mla_kv_norm[l][None], wkv,
            rope_s, rope_m, tiled)
        r3 = lambda a: a.reshape(bsz, seq, a.shape[-1])
        o_at = _na_call(r3(aq), r3(ak), av, _na_bias_tables(na_rpb[l]))
        o_bt = _mla_call(r3(mq), r3(mk), mv)
        o_ct = _swa_call(swa_sink[l], r3(sq), r3(sk), sv)

        gn = group_norm[l]
        wo = w_out[l]
        gc = gn[A_W + B_W:][c_perm]
        wc = wo[A_W + B_W:][c_perm]
        hm, xn3, aff_t = _out_call(
            o_at, o_bt, o_ct, h2,
            gn[:A_W, None], gn[A_W:A_W + B_W, None], gc[:, None],
            wo[:A_W].astype(BF16), wo[A_W:A_W + B_W].astype(BF16), wc.astype(BF16),
            ffn_norm[l][None], w_router[l].T, tiled)

        idx, gate = _topk_call(aff_t.reshape(bsz * N_EXPERTS, seq))
        idx_flat = idx.reshape(-1)
        y = _ffn_call(idx_flat, xn3, w_gate, w_up, w_down, l)
        h2 = _combine_call(idx_flat, gate.reshape(-1), y,
                           hm.reshape(bsz, seq * TOKEN_SUB, LANES)).reshape(n * TOKEN_SUB, LANES)
    return _norm_call(h2, final_norm[None]).reshape(bsz, seq, d)
```

```python
import functools

import jax
import jax.numpy as jnp
import numpy as np
from jax import lax
from jax.experimental import pallas as pl
from jax.experimental.pallas import tpu as pltpu

F32 = jnp.float32
BF16 = jnp.bfloat16

D_MODEL = 1024
SEQ = 4096
HEAD_DIM = 64
GRID_W = 64
NA_HEADS = 4
NA_WIN_H = 8
NA_WIN_W = 16
MLA_HEADS = 6
MLA_Q_LORA = 256
MLA_KV_LORA = 128
MLA_NOPE = 64
MLA_ROPE = 32
MLA_V = 64
SWA_HEADS = 6
SWA_KV_HEADS = 2
SWA_GROUP = SWA_HEADS // SWA_KV_HEADS
SWA_WINDOW = 128
ROPE_THETA = 10000.0
N_EXPERTS = 16
EC_CAPACITY = 2
CAP = EC_CAPACITY * SEQ // N_EXPERTS
RMS_EPS = 1e-6
NEG_INF = -1e30

A_W = NA_HEADS * HEAD_DIM
B_W = MLA_HEADS * MLA_V
C_W = SWA_HEADS * HEAD_DIM

LANES = 128
TOKEN_SUB = D_MODEL // LANES
ROW_TILE = 512
VMEM_LIMIT = 56 * 1024 * 1024

P_AQ, P_AK, P_AV = 0, 256, 512
P_CQ, P_CKV = 768, 1024
P_SQ, P_SK, P_SV = 1152, 1536, 1664
P_KR = 1792
P_COLS = 1920

NA_QROWS = 4
NA_KROWS = 12
NA_TQ = NA_QROWS * GRID_W
NA_TK = NA_KROWS * GRID_W
NA_STEP_BLOCKS = 2
MLA_TQ = 512
MLA_TK = 256
VT_PAD = 16
MLA_VT_ROWS = MLA_V + VT_PAD
VT_ROWS = HEAD_DIM + VT_PAD
MLA_AHEAD = 3
WIN_AHEAD = 3
LOG2_E = 1.4426950408889634
FFN_CHUNKS = 4
SWA_TQ = 1024
SWA_SUB = 256
SWA_TK = SWA_SUB + 2 * SWA_WINDOW


def _cparams(sem):
    return pltpu.CompilerParams(dimension_semantics=sem, vmem_limit_bytes=VMEM_LIMIT)


def _rms(x, gain):
    ms = jnp.mean(x * x, axis=-1, keepdims=True)
    return x * lax.rsqrt(ms + RMS_EPS) * gain


def _load_token_rows(ref, rows):
    return jnp.concatenate([ref[pl.ds(s, rows, stride=TOKEN_SUB), :] for s in range(TOKEN_SUB)], axis=1)


def _store_token_rows(ref, val):
    rows = val.shape[0]
    for s in range(TOKEN_SUB):
        ref[pl.ds(s, rows, stride=TOKEN_SUB), :] = val[:, s * LANES:(s + 1) * LANES]


def _nt_dot(a, b):
    return lax.dot_general(a, b, (((1,), (1,)), ((), ())), preferred_element_type=F32)


def _proj_kernel(x_ref, gain_ref, win_ref, qn_ref, wq_ref, kvn_ref, wkv_ref, rs_ref, rm_ref,
                 aq_ref, ak_ref, av_ref, mq_ref, mk_ref, mv_ref, sq_ref, sk_ref, sv_ref, *, tiled):
    x = _load_token_rows(x_ref, ROW_TILE) if tiled else x_ref[...]
    xn = _rms(x, gain_ref[...]).astype(BF16)
    proj = jnp.dot(xn, win_ref[...], preferred_element_type=F32)
    lane = lax.broadcasted_iota(jnp.int32, (1, LANES), 1)
    lo_half = lane < HEAD_DIM

    def rope(grp, tab_ref, half):
        c = tab_ref[:, 0:LANES]
        sa = tab_ref[:, LANES:2 * LANES]
        sb = tab_ref[:, 2 * LANES:3 * LANES]
        return grp * c + pltpu.roll(grp, half, 1) * sa + pltpu.roll(grp, LANES - half, 1) * sb

    def split_heads(grp):
        zero = jnp.zeros_like(grp)
        return jnp.where(lo_half, grp, zero).astype(BF16), jnp.where(lo_half, zero, grp).astype(BF16)

    def store_values_transposed(ref, vals, heads, width):
        vt = vals.T
        ones_row = jnp.where(lax.broadcasted_iota(jnp.int32, (VT_PAD, vt.shape[1]), 0) == 0, 1.0, 0.0)
        for h in range(heads):
            blk = jnp.concatenate([vt[h * width:(h + 1) * width], ones_row], axis=0)
            ref[h * (width + VT_PAD):(h + 1) * (width + VT_PAD), :] = blk.astype(BF16)

    na_scale = HEAD_DIM ** -0.5 * LOG2_E
    for p in range(NA_HEADS // 2):
        grp = proj[:, P_AQ + p * LANES:P_AQ + (p + 1) * LANES] * na_scale
        q0, q1 = split_heads(grp)
        aq_ref[:, (2 * p) * LANES:(2 * p + 1) * LANES] = q0
        aq_ref[:, (2 * p + 1) * LANES:(2 * p + 2) * LANES] = q1
    ak_ref[...] = proj[:, P_AK:P_AK + A_W].astype(BF16)
    store_values_transposed(av_ref, proj[:, P_AV:P_AV + A_W], NA_HEADS, HEAD_DIM)

    cq = _rms(proj[:, P_CQ:P_CQ + MLA_Q_LORA], qn_ref[...]).astype(BF16)
    q = jnp.dot(cq, wq_ref[...], preferred_element_type=F32) * ((MLA_NOPE + MLA_ROPE) ** -0.5 * LOG2_E)
    ckv = _rms(proj[:, P_CKV:P_CKV + MLA_KV_LORA], kvn_ref[...]).astype(BF16)
    kv = jnp.dot(ckv, wkv_ref[...], preferred_element_type=F32)
    kr = rope(proj[:, P_KR:P_KR + LANES], rm_ref, MLA_ROPE // 2)
    for h in range(MLA_HEADS):
        sl = slice(h * LANES, (h + 1) * LANES)
        mq_ref[:, sl] = rope(q[:, sl], rm_ref, MLA_ROPE // 2).astype(BF16)
        mk_ref[:, sl] = (kv[:, sl] + kr).astype(BF16)
    store_values_transposed(mv_ref, kv[:, MLA_HEADS * LANES:MLA_HEADS * LANES + B_W], MLA_HEADS, MLA_V)

    swa_scale = HEAD_DIM ** -0.5 * LOG2_E
    for g in range(SWA_GROUP):
        grp = rope(proj[:, P_SQ + g * LANES:P_SQ + (g + 1) * LANES], rs_ref, HEAD_DIM // 2) * swa_scale
        q0, q1 = split_heads(grp)
        sq_ref[:, (2 * g) * LANES:(2 * g + 1) * LANES] = q0
        sq_ref[:, (2 * g + 1) * LANES:(2 * g + 2) * LANES] = q1
    sk_ref[...] = rope(proj[:, P_SK:P_SK + LANES], rs_ref, HEAD_DIM // 2).astype(BF16)
    store_values_transposed(sv_ref, proj[:, P_SV:P_SV + LANES], SWA_KV_HEADS, HEAD_DIM)


def _proj_call(x2, gain, win, qn, wq, kvn, wkv, rope_s, rope_m, tiled):
    tm = ROW_TILE
    n = x2.shape[0] // TOKEN_SUB if tiled else x2.shape[0]
    xspec = pl.BlockSpec((tm * TOKEN_SUB, LANES) if tiled else (tm, D_MODEL), lambda i: (i, 0))
    seq_tiles = SEQ // tm
    row = lambda w: pl.BlockSpec((tm, w), lambda i: (i, 0))
    full = lambda a: pl.BlockSpec(a.shape, lambda i: (0,) * a.ndim)
    pos = lambda w: pl.BlockSpec((tm, w), lambda i: (i % seq_tiles, 0))
    outs = ((2 * A_W, False), (A_W, False), (NA_HEADS * VT_ROWS, True),
            (MLA_HEADS * LANES, False), (MLA_HEADS * LANES, False), (MLA_HEADS * MLA_VT_ROWS, True),
            (2 * C_W, False), (LANES, False), (SWA_KV_HEADS * VT_ROWS, True))
    out_specs = [pl.BlockSpec((w, tm), lambda i: (0, i)) if t else row(w) for w, t in outs]
    out_shape = [jax.ShapeDtypeStruct((w, n) if t else (n, w), BF16) for w, t in outs]
    return pl.pallas_call(
        functools.partial(_proj_kernel, tiled=tiled),
        grid=(n // tm,),
        in_specs=[xspec, full(gain), full(win), full(qn), full(wq), full(kvn), full(wkv),
                  pos(3 * LANES), pos(3 * LANES)],
        out_specs=out_specs,
        out_shape=out_shape,
        compiler_params=_cparams(("arbitrary",)),
        name="norm_in_proj",
    )(x2, gain, win, qn, wq, kvn, wkv, rope_s, rope_m)


def _na_key_start(j):
    rows = SEQ // GRID_W
    return jnp.clip(NA_QROWS * j - NA_WIN_H // 2, 0, rows - NA_KROWS)


def _na_kernel(q_ref, k_ref, vt_ref, bias0_ref, bias1_ref, o_ref):
    bias_refs = (bias0_ref, bias1_ref)
    starts, kwins = [], []
    for jj in range(NA_STEP_BLOCKS):
        ks = pl.multiple_of(_na_key_start(NA_STEP_BLOCKS * pl.program_id(1) + jj) * GRID_W, 256)
        starts.append(ks)
        kwins.append(k_ref[0, pl.ds(ks, NA_TK), :])

    def scores(item):
        jj, h = divmod(item, NA_HEADS)
        p = h // 2
        q = q_ref[0, jj * NA_TQ:(jj + 1) * NA_TQ, h * LANES:(h + 1) * LANES]
        return _nt_dot(kwins[jj][:, p * LANES:(p + 1) * LANES], q) + bias_refs[jj][0, h]

    items = NA_STEP_BLOCKS * NA_HEADS
    ahead = [scores(it) for it in range(WIN_AHEAD)]
    for item in range(items):
        jj, h = divmod(item, NA_HEADS)
        st = ahead.pop(0)
        if item + WIN_AHEAD < items:
            ahead.append(scores(item + WIN_AHEAD))
        m = jnp.max(st, axis=0, keepdims=True)
        e = jnp.exp2(st - m).astype(BF16)
        pv = jnp.dot(vt_ref[h * VT_ROWS:(h + 1) * VT_ROWS, pl.ds(starts[jj], NA_TK)], e,
                     preferred_element_type=F32)
        out = pv[:HEAD_DIM] / pv[HEAD_DIM:HEAD_DIM + 1]
        o_ref[h * HEAD_DIM:(h + 1) * HEAD_DIM, jj * NA_TQ:(jj + 1) * NA_TQ] = out.astype(o_ref.dtype)


def _na_call(aq, ak, avt, bias_t):
    b = aq.shape[0]
    nj = SEQ // NA_TQ
    steps = nj // NA_STEP_BLOCKS

    def variant(jj):
        def index_map(bi, i):
            j = NA_STEP_BLOCKS * i + jj
            return (jnp.where(j == 0, 0, jnp.where(j == nj - 1, 2, 1)), 0, 0, 0)
        return index_map

    bias_spec = lambda jj: pl.BlockSpec((1, NA_HEADS, NA_TK, NA_TQ), variant(jj))
    tq = NA_STEP_BLOCKS * NA_TQ
    return pl.pallas_call(
        _na_kernel,
        grid=(b, steps),
        in_specs=[pl.BlockSpec((1, tq, 2 * A_W), lambda bi, i: (bi, i, 0)),
                  pl.BlockSpec((1, SEQ, A_W), lambda bi, i: (bi, 0, 0)),
                  pl.BlockSpec((NA_HEADS * VT_ROWS, SEQ), lambda bi, i: (0, bi)),
                  bias_spec(0), bias_spec(1)],
        out_specs=pl.BlockSpec((A_W, tq), lambda bi, i: (0, bi * steps + i)),
        out_shape=jax.ShapeDtypeStruct((A_W, b * SEQ), BF16),
        compiler_params=_cparams(("arbitrary", "arbitrary")),
        name="neighbourhood_attn",
    )(aq, ak, avt, bias_t, bias_t)


def _na_bias_tables(rpb):
    rows = SEQ // GRID_W
    nj = SEQ // NA_TQ
    c = np.arange(GRID_W)[:, None]
    kc = np.arange(GRID_W)[None, :]
    cs = np.clip(c - NA_WIN_W // 2, 0, GRID_W - NA_WIN_W)
    cvalid = (kc >= cs) & (kc < cs + NA_WIN_W)
    pad = GRID_W - NA_WIN_W
    padded = jnp.pad(rpb, ((0, 0), (0, 0), (pad, pad)))
    toep = jnp.stack([padded[:, :, GRID_W - 1 - ci:2 * GRID_W - 1 - ci] for ci in range(GRID_W)], axis=2)
    toep = jnp.where(cvalid[None, None], toep * LOG2_E, NEG_INF)
    toep_t = jnp.swapaxes(toep, 2, 3)
    masked = jnp.full((rpb.shape[0], GRID_W, GRID_W), NEG_INF, F32)
    variants = []
    for j in (0, 1, nj - 1):
        ks = int(np.clip(NA_QROWS * j - NA_WIN_H // 2, 0, rows - NA_KROWS))
        key_blocks = []
        for u in range(NA_KROWS):
            kr = ks + u
            blocks = []
            for i in range(NA_QROWS):
                r = NA_QROWS * j + i
                rs = int(np.clip(r - NA_WIN_H // 2, 0, rows - NA_WIN_H))
                blocks.append(toep_t[:, kr - r + NA_WIN_H - 1] if rs <= kr < rs + NA_WIN_H else masked)
            key_blocks.append(jnp.concatenate(blocks, axis=2))
        variants.append(jnp.concatenate(key_blocks, axis=1))
    return jnp.stack(variants)


def _mla_kernel(q_ref, k_ref, vt_ref, o_ref):
    heads = range(2)
    nchunks = SEQ // MLA_TK

    def scores(c, hh):
        sl = slice(hh * LANES, (hh + 1) * LANES)
        return _nt_dot(k_ref[0, c * MLA_TK:(c + 1) * MLA_TK, sl], q_ref[0, :, sl])

    m = [jnp.full((1, MLA_TQ), NEG_INF, F32) for _ in heads]
    acc = [jnp.zeros((MLA_VT_ROWS, MLA_TQ), F32) for _ in heads]
    items = [(c, hh) for c in range(nchunks) for hh in heads]
    ahead = [scores(*it) for it in items[:MLA_AHEAD]]
    for n, (c, hh) in enumerate(items):
        st = ahead.pop(0)
        if n + MLA_AHEAD < len(items):
            ahead.append(scores(*items[n + MLA_AHEAD]))
        m_new = jnp.maximum(m[hh], jnp.max(st, axis=0, keepdims=True))
        alpha = jnp.exp2(m[hh] - m_new)
        e = jnp.exp2(st - m_new).astype(BF16)
        vt = vt_ref[hh * MLA_VT_ROWS:(hh + 1) * MLA_VT_ROWS, c * MLA_TK:(c + 1) * MLA_TK]
        acc[hh] = alpha * acc[hh] + jnp.dot(vt, e, preferred_element_type=F32)
        m[hh] = m_new
    out = jnp.concatenate([acc[hh][:MLA_V] / acc[hh][MLA_V:MLA_V + 1] for hh in heads], axis=0)
    o_ref[...] = out.astype(o_ref.dtype)


def _mla_call(mq, mk, mvt):
    b = mq.shape[0]
    pairs = MLA_HEADS // 2
    nq = SEQ // MLA_TQ
    return pl.pallas_call(
        _mla_kernel,
        grid=(b, pairs, nq),
        in_specs=[pl.BlockSpec((1, MLA_TQ, 2 * LANES), lambda bi, p, i: (bi, i, p)),
                  pl.BlockSpec((1, SEQ, 2 * LANES), lambda bi, p, i: (bi, 0, p)),
                  pl.BlockSpec((2 * MLA_VT_ROWS, SEQ), lambda bi, p, i: (p, bi))],
        out_specs=pl.BlockSpec((2 * MLA_V, MLA_TQ), lambda bi, p, i: (p, bi * nq + i)),
        out_shape=jax.ShapeDtypeStruct((B_W, b * SEQ), BF16),
        compiler_params=_cparams(("arbitrary", "arbitrary", "arbitrary")),
        name="latent_attn",
    )(mq, mk, mvt)


def _swa_kernel(sink_ref, q_ref, k_ref, vt_ref, o_ref):
    i = pl.program_id(1)
    nsub = SWA_TQ // SWA_SUB
    starts, kwins, valids = [], [], []
    for sb in range(nsub):
        q0 = SWA_TQ * i + sb * SWA_SUB
        ws = pl.multiple_of(jnp.clip(q0 - SWA_WINDOW, 0, SEQ - SWA_TK), LANES)
        kpos = ws + lax.broadcasted_iota(jnp.int32, (SWA_TK, 1), 0)
        qpos = q0 + lax.broadcasted_iota(jnp.int32, (1, SWA_SUB), 1)
        starts.append(ws)
        kwins.append(k_ref[0, pl.ds(ws, SWA_TK), :])
        valids.append(jnp.abs(qpos - kpos) <= SWA_WINDOW)

    def scores(item):
        sb, grp = divmod(item, SWA_HEADS)
        st = _nt_dot(kwins[sb], q_ref[0, sb * SWA_SUB:(sb + 1) * SWA_SUB, grp * LANES:(grp + 1) * LANES])
        return jnp.where(valids[sb], st, NEG_INF)

    items = nsub * SWA_HEADS
    ahead = [scores(it) for it in range(WIN_AHEAD)]
    for item in range(items):
        sb, grp = divmod(item, SWA_HEADS)
        g, kvh = divmod(grp, SWA_KV_HEADS)
        st = ahead.pop(0)
        if item + WIN_AHEAD < items:
            ahead.append(scores(item + WIN_AHEAD))
        sink = sink_ref[kvh * SWA_GROUP + g] * LOG2_E
        m = jnp.maximum(jnp.max(st, axis=0, keepdims=True), sink)
        e = jnp.exp2(st - m).astype(BF16)
        pv = jnp.dot(vt_ref[kvh * VT_ROWS:(kvh + 1) * VT_ROWS, pl.ds(starts[sb], SWA_TK)], e,
                     preferred_element_type=F32)
        l = pv[HEAD_DIM:HEAD_DIM + 1] + jnp.exp2(sink - m)
        out = pv[:HEAD_DIM] / l
        o_ref[grp * HEAD_DIM:(grp + 1) * HEAD_DIM, sb * SWA_SUB:(sb + 1) * SWA_SUB] = out.astype(o_ref.dtype)


def _swa_call(sink, sq, sk, svt):
    b = sq.shape[0]
    ni = SEQ // SWA_TQ
    return pl.pallas_call(
        _swa_kernel,
        grid=(b, ni),
        in_specs=[pl.BlockSpec(memory_space=pltpu.SMEM),
                  pl.BlockSpec((1, SWA_TQ, 2 * C_W), lambda bi, i: (bi, i, 0)),
                  pl.BlockSpec((1, SEQ, LANES), lambda bi, i: (bi, 0, 0)),
                  pl.BlockSpec((SWA_KV_HEADS * VT_ROWS, SEQ), lambda bi, i: (0, bi))],
        out_specs=pl.BlockSpec((C_W, SWA_TQ), lambda bi, i: (0, bi * ni + i)),
        out_shape=jax.ShapeDtypeStruct((C_W, b * SEQ), BF16),
        compiler_params=_cparams(("arbitrary", "arbitrary")),
        name="window_gqa",
    )(sink, sq, sk, svt)


def _out_kernel(oa_ref, ob_ref, oc_ref, h_ref, ga_ref, gb_ref, gc_ref, wa_ref, wb_ref, wc_ref,
                gf_ref, wr_ref, hm_ref, xn_ref, aff_ref, *, tiled):
    acc = _load_token_rows(h_ref, ROW_TILE) if tiled else h_ref[...]
    for o_ref, g_ref, w_ref in ((oa_ref, ga_ref, wa_ref), (ob_ref, gb_ref, wb_ref), (oc_ref, gc_ref, wc_ref)):
        o = o_ref[...].astype(F32)
        ms = jnp.mean(o * o, axis=0, keepdims=True)
        on = (o * lax.rsqrt(ms + RMS_EPS) * g_ref[...]).astype(BF16)
        acc = acc + lax.dot_general(on, w_ref[...], (((0,), (0,)), ((), ())), preferred_element_type=F32)
    _store_token_rows(hm_ref, acc)
    xn = _rms(acc, gf_ref[...])
    _store_token_rows(xn_ref, xn)
    wr = wr_ref[...]
    wr_hi = wr.astype(BF16)
    wr_lo = (wr - wr_hi.astype(F32)).astype(BF16)
    xn_hi = xn.astype(BF16)
    xn_lo = (xn - xn_hi.astype(F32)).astype(BF16)
    logits = _nt_dot(wr_hi, xn_hi) + (_nt_dot(wr_hi, xn_lo) + _nt_dot(wr_lo, xn_hi))
    m = jnp.max(logits, axis=0, keepdims=True)
    e = jnp.exp(logits - m)
    aff_ref[0] = e / jnp.sum(e, axis=0, keepdims=True)


def _out_call(oa, ob, oc, h2, ga, gb, gc, wa, wb, wc, gf, wr_t, tiled):
    n = oa.shape[1]
    tm = ROW_TILE
    col = lambda w: pl.BlockSpec((w, tm), lambda i: (0, i))
    tok = pl.BlockSpec((tm * TOKEN_SUB, LANES), lambda i: (i, 0))
    hspec = tok if tiled else pl.BlockSpec((tm, D_MODEL), lambda i: (i, 0))
    seq_tiles = SEQ // tm
    row = lambda w: pl.BlockSpec((tm, w), lambda i: (i, 0))
    full = lambda a: pl.BlockSpec(a.shape, lambda i: (0,) * a.ndim)
    return pl.pallas_call(
        functools.partial(_out_kernel, tiled=tiled),
        grid=(n // tm,),
        in_specs=[col(A_W), col(B_W), col(C_W), hspec, full(ga), full(gb), full(gc),
                  full(wa), full(wb), full(wc), full(gf), full(wr_t)],
        out_specs=[tok, tok,
                   pl.BlockSpec((1, N_EXPERTS, tm), lambda i: (i // seq_tiles, 0, i % seq_tiles))],
        out_shape=[jax.ShapeDtypeStruct((n * TOKEN_SUB, LANES), F32), jax.ShapeDtypeStruct((n * TOKEN_SUB, LANES), F32),
                   jax.ShapeDtypeStruct((n // SEQ, N_EXPERTS, SEQ), F32)],
        compiler_params=_cparams(("arbitrary",)),
        name="out_proj_router",
    )(oa, ob, oc, h2, ga, gb, gc, wa, wb, wc, gf, wr_t)


def _topk_kernel(aff_ref, idx_ref, gate_ref, posl_ref, affr_ref):
    rows, seq = aff_ref.shape
    tiles = seq // LANES
    a = aff_ref[...]
    int_min = jnp.int32(-2 ** 31)

    def ordered_to_float(u):
        key = u ^ int_min
        bits = key ^ (lax.shift_right_arithmetic(key, jnp.int32(31)) & jnp.int32(0x7FFFFFFF))
        return lax.bitcast_convert_type(bits, F32)

    t_u = jnp.zeros((rows, 1), jnp.int32)
    for bit in range(31, -1, -1):
        step = int_min if bit == 31 else jnp.int32(1 << bit)
        cand_u = t_u | step
        cnt = jnp.sum(jnp.where(a >= ordered_to_float(cand_u), 1.0, 0.0), axis=1, keepdims=True)
        t_u = jnp.where(cnt >= CAP, cand_u, t_u)
    thr = ordered_to_float(t_u)
    gt = a > thr
    eq = a == thr
    need = CAP - jnp.sum(jnp.where(gt, 1.0, 0.0), axis=1, keepdims=True)

    tri = jnp.where(lax.broadcasted_iota(jnp.int32, (LANES, LANES), 0)
                    <= lax.broadcasted_iota(jnp.int32, (LANES, LANES), 1), 1.0, 0.0).astype(BF16)

    def prefix_incl(flags_f32, t, carry):
        blk = flags_f32[:, t * LANES:(t + 1) * LANES]
        inc = jnp.dot(blk.astype(BF16), tri, preferred_element_type=F32) + carry
        return blk, inc, inc[:, LANES - 1:LANES]

    eq_f = jnp.where(eq, 1.0, 0.0)
    gt_f = jnp.where(gt, 1.0, 0.0)
    carry_eq = jnp.zeros((rows, 1), F32)
    for t in range(tiles):
        eq_blk, eq_inc, carry_eq = prefix_incl(eq_f, t, carry_eq)
        sel_blk = jnp.maximum(gt_f[:, t * LANES:(t + 1) * LANES],
                              jnp.where(eq_inc <= need, eq_blk, 0.0))
        sel_loc = jnp.dot(sel_blk.astype(BF16), tri, preferred_element_type=F32)
        posl_ref[t * rows:(t + 1) * rows, :] = jnp.where(sel_blk > 0.0, sel_loc - 1.0, -1.0)
        affr_ref[t * rows:(t + 1) * rows, :] = a[:, t * LANES:(t + 1) * LANES]

    slot = lax.broadcasted_iota(jnp.int32, (1, CAP), 1).astype(F32)
    tile_id = lax.broadcasted_iota(jnp.int32, (tiles, 1), 0).astype(F32)
    lane_id = lax.broadcasted_iota(jnp.int32, (LANES, 1), 0).astype(F32)
    before = jnp.where(lax.broadcasted_iota(jnp.int32, (tiles, tiles), 1)
                       < lax.broadcasted_iota(jnp.int32, (tiles, tiles), 0), 1.0, 0.0).astype(BF16)
    sub8 = lax.broadcasted_iota(jnp.int32, (8, CAP), 0)
    tn_dot = lambda x, onehot: lax.dot_general(x.astype(BF16), onehot, (((0,), (0,)), ((), ())),
                                               preferred_element_type=F32)

    def row_group(i8, carry):
        r8 = pl.multiple_of(i8 * 8, 8)
        idx8 = jnp.zeros((8, CAP), F32)
        gate8 = jnp.zeros((8, CAP), F32)
        for k in range(8):
            i = r8 + k
            ranks = posl_ref[pl.ds(i, tiles, stride=rows), :]
            affs = affr_ref[pl.ds(i, tiles, stride=rows), :]
            count = jnp.max(ranks, axis=1, keepdims=True) + 1.0
            start = jnp.dot(before, jnp.broadcast_to(count, (tiles, LANES)).astype(BF16),
                            preferred_element_type=F32)[:, 0:1]
            tile_of = jnp.sum(jnp.where(start <= slot, 1.0, 0.0), axis=0, keepdims=True) - 1.0
            pick = tile_id == tile_of
            onehot = jnp.where(pick, 1.0, 0.0).astype(BF16)
            rank = slot - jnp.sum(jnp.where(pick, start, 0.0), axis=0, keepdims=True)
            match = tn_dot(ranks, onehot) == rank
            g0 = affs.astype(BF16).astype(F32)
            r1 = affs - g0
            g1 = r1.astype(BF16).astype(F32)
            vals = tn_dot(g0, onehot) + tn_dot(g1, onehot) + tn_dot(r1 - g1, onehot)
            tok = tile_of * LANES + jnp.sum(jnp.where(match, lane_id, 0.0), axis=0, keepdims=True)
            tok = tok * TOKEN_SUB
            gate = jnp.sum(jnp.where(match, vals, 0.0), axis=0, keepdims=True)
            idx8 = jnp.where(sub8 == k, jnp.broadcast_to(tok, (8, CAP)), idx8)
            gate8 = jnp.where(sub8 == k, jnp.broadcast_to(gate, (8, CAP)), gate8)
        idx_ref[pl.ds(r8, 8), :] = idx8.astype(jnp.int32)
        gate_ref[pl.ds(r8, 8), :] = gate8
        return carry

    lax.fori_loop(0, rows // 8, row_group, 0)


def _topk_call(aff2):
    rows, seq = aff2.shape
    return pl.pallas_call(
        _topk_kernel,
        out_shape=[jax.ShapeDtypeStruct((rows, CAP), jnp.int32), jax.ShapeDtypeStruct((rows, CAP), F32)],
        scratch_shapes=[pltpu.VMEM((seq // LANES * rows, LANES), F32), pltpu.VMEM((seq // LANES * rows, LANES), F32)],
        compiler_params=pltpu.CompilerParams(vmem_limit_bytes=VMEM_LIMIT),
        name="expert_choice_topk",
    )(aff2)


def _ffn_kernel(row_ref, x_hbm, wg_ref, wu_ref, wd_ref, y_ref, xs_ref, wbf_ref, sem_ref):
    e = pl.program_id(0)
    b = pl.program_id(1)
    nb = pl.num_programs(1)
    steps = N_EXPERTS * nb
    t = e * nb + b

    def issue(step, slot):
        sb = step % nb
        base = (sb * N_EXPERTS + step // nb) * CAP
        row0 = sb * (SEQ * TOKEN_SUB)

        def body(c, carry):
            src = pl.multiple_of(row0 + row_ref[base + c], TOKEN_SUB)
            dst = pl.multiple_of(c * TOKEN_SUB, TOKEN_SUB)
            pltpu.make_async_copy(x_hbm.at[pl.ds(src, TOKEN_SUB), :], xs_ref.at[slot, pl.ds(dst, TOKEN_SUB), :],
                                  sem_ref.at[slot]).start()
            return carry

        lax.fori_loop(0, CAP, body, 0, unroll=8)

    @pl.when(t == 0)
    def _():
        issue(t, 0)

    @pl.when(b == 0)
    def _():
        wbf_ref[0] = wg_ref[0, 0].astype(BF16)
        wbf_ref[1] = wu_ref[0, 0].astype(BF16)
        wbf_ref[2] = wd_ref[0, 0].astype(BF16)

    slot = t % 2
    wait_all = lambda s: pltpu.make_async_copy(x_hbm.at[pl.ds(0, CAP * TOKEN_SUB), :], xs_ref.at[s],
                                               sem_ref.at[s]).wait()
    wait_all(slot)
    xs = _load_token_rows(xs_ref.at[slot], CAP).astype(BF16)

    nstep = jnp.where(t + 1 < steps, t + 1, 0)
    nslot = 1 - slot
    nsb = nstep % nb
    nbase = (nsb * N_EXPERTS + nstep // nb) * CAP
    nrow0 = nsb * (SEQ * TOKEN_SUB)
    pieces = 3 * FFN_CHUNKS
    bounds = [CAP * k // pieces for k in range(pieces + 1)]
    piece = iter(range(pieces))

    def issue_piece():
        k = next(piece)
        for c in range(bounds[k], bounds[k + 1]):
            src = pl.multiple_of(nrow0 + row_ref[nbase + c], TOKEN_SUB)
            pltpu.make_async_copy(x_hbm.at[pl.ds(src, TOKEN_SUB), :],
                                  xs_ref.at[nslot, pl.ds(c * TOKEN_SUB, TOKEN_SUB), :], sem_ref.at[nslot]
                                  ).start(priority=c % 2)

    width = D_MODEL // FFN_CHUNKS
    y = None
    for j in range(FFN_CHUNKS):
        cols = slice(j * width, (j + 1) * width)
        gate = jnp.dot(xs, wbf_ref[0, :, cols], preferred_element_type=F32)
        issue_piece()
        up = jnp.dot(xs, wbf_ref[1, :, cols], preferred_element_type=F32)
        issue_piece()
        hid = (gate * (1.0 / (1.0 + jnp.exp(-gate))) * up).astype(BF16)
        part = jnp.dot(hid, wbf_ref[2, cols, :], preferred_element_type=F32)
        y = part if y is None else y + part
        issue_piece()
    _store_token_rows(y_ref.at[0, 0], y)

    @pl.when(t == steps - 1)
    def _():
        wait_all(nslot)


def _ffn_call(idx_flat, xn3, wg, wu, wd, layer):
    b = xn3.shape[0] // (SEQ * TOKEN_SUB)
    wspec = pl.BlockSpec((1, 1, D_MODEL, D_MODEL), lambda e, bi, idx: (layer, e, 0, 0))
    return pl.pallas_call(
        _ffn_kernel,
        grid_spec=pltpu.PrefetchScalarGridSpec(
            num_scalar_prefetch=1,
            grid=(N_EXPERTS, b),
            in_specs=[pl.BlockSpec(memory_space=pl.ANY), wspec, wspec, wspec],
            out_specs=pl.BlockSpec((1, 1, CAP * TOKEN_SUB, LANES), lambda e, bi, idx: (bi, e, 0, 0)),
            scratch_shapes=[pltpu.VMEM((2, CAP * TOKEN_SUB, LANES), F32),
                            pltpu.VMEM((3, D_MODEL, D_MODEL), BF16),
                            pltpu.SemaphoreType.DMA((2,))],
        ),
        out_shape=jax.ShapeDtypeStruct((b, N_EXPERTS, CAP * TOKEN_SUB, LANES), F32),
        compiler_params=_cparams(("arbitrary", "arbitrary")),
        name="expert_ffn",
    )(idx_flat, xn3, wg, wu, wd)


def _combine_kernel(row_ref, gate_ref, y_ref, h_hbm, o_hbm, acc_ref, sem_ref):
    b = pl.program_id(0)
    e = pl.program_id(1)
    nb = pl.num_programs(0)
    slot = b % 2
    other = 1 - slot
    load = lambda bb, s: pltpu.make_async_copy(h_hbm.at[bb], acc_ref.at[s], sem_ref.at[0, s])
    drain = lambda bb, s: pltpu.make_async_copy(acc_ref.at[s], o_hbm.at[bb], sem_ref.at[1, s])

    @pl.when((b == 0) & (e == 0))
    def _():
        load(b, slot).start()

    @pl.when(e == 0)
    def _():
        load(b, slot).wait()

    @pl.when((e == 1) & (b >= 1))
    def _():
        drain(b - 1, other).wait()

    @pl.when((e == 1) & (b + 1 < nb))
    def _():
        load(b + 1, other).start()

    base = (b * N_EXPERTS + e) * CAP
    group = 8

    def scatter_add(acc):
        def body(cg, carry):
            first = base + cg * group
            src0 = pl.multiple_of(cg * (group * TOKEN_SUB), group * TOKEN_SUB)
            new = []
            for k in range(group):
                dst = pl.multiple_of(row_ref[first + k], TOKEN_SUB)
                new.append((dst, acc[pl.ds(dst, TOKEN_SUB), :]
                            + y_ref[0, 0, pl.ds(src0 + k * TOKEN_SUB, TOKEN_SUB), :] * gate_ref[first + k]))
            for dst, val in new:
                acc[pl.ds(dst, TOKEN_SUB), :] = val
            return carry

        lax.fori_loop(0, CAP // group, body, 0)

    for s in range(2):
        pl.when(slot == s)(functools.partial(scatter_add, acc_ref.at[s]))

    @pl.when(e == N_EXPERTS - 1)
    def _():
        drain(b, slot).start()

    @pl.when((e == N_EXPERTS - 1) & (b == nb - 1))
    def _():
        drain(b, slot).wait()


def _combine_call(idx_flat, gate_flat, y, h3):
    b = h3.shape[0]
    return pl.pallas_call(
        _combine_kernel,
        grid_spec=pltpu.PrefetchScalarGridSpec(
            num_scalar_prefetch=2,
            grid=(b, N_EXPERTS),
            in_specs=[pl.BlockSpec((1, 1, CAP * TOKEN_SUB, LANES), lambda bi, e, idx, gt: (bi, e, 0, 0)),
                      pl.BlockSpec(memory_space=pl.ANY)],
            out_specs=pl.BlockSpec(memory_space=pl.ANY),
            scratch_shapes=[pltpu.VMEM((2, SEQ * TOKEN_SUB, LANES), F32), pltpu.SemaphoreType.DMA((2, 2))],
        ),
        out_shape=jax.ShapeDtypeStruct(h3.shape, F32),
        compiler_params=_cparams(("arbitrary", "arbitrary")),
        name="expert_combine",
    )(idx_flat, gate_flat, y, h3)


def _norm_kernel(x_ref, g_ref, o_ref):
    o_ref[...] = _rms(_load_token_rows(x_ref, ROW_TILE), g_ref[...])


def _norm_call(x3, gain):
    n = x3.shape[0] // TOKEN_SUB
    tm = ROW_TILE
    return pl.pallas_call(
        _norm_kernel,
        grid=(n // tm,),
        in_specs=[pl.BlockSpec((tm * TOKEN_SUB, LANES), lambda i: (i, 0)),
                  pl.BlockSpec((1, D_MODEL), lambda i: (0, 0))],
        out_specs=pl.BlockSpec((tm, D_MODEL), lambda i: (i, 0)),
        out_shape=jax.ShapeDtypeStruct((n, D_MODEL), F32),
        compiler_params=_cparams(("arbitrary",)),
        name="final_norm",
    )(x3, gain)


def _rope_table(dim, lead):
    half = dim // 2
    inv = 1.0 / (ROPE_THETA ** (jnp.arange(0, dim, 2, dtype=F32) / dim))
    ang = jnp.arange(SEQ, dtype=F32)[:, None] * inv[None, :]
    cos, sin = jnp.cos(ang), jnp.sin(ang)
    zero = jnp.zeros_like(sin)
    if lead:
        tail = jnp.zeros((SEQ, LANES - lead - dim), F32)
        ones = jnp.ones((SEQ, lead), F32)
        zl = jnp.zeros((SEQ, lead), F32)
        c = jnp.concatenate([ones, cos, cos, tail], axis=1)
        sa = jnp.concatenate([zl, zero, sin, tail], axis=1)
        sb = jnp.concatenate([zl, -sin, zero, tail], axis=1)
    else:
        reps = LANES // dim
        c = jnp.concatenate([cos, cos] * reps, axis=1)
        sa = jnp.concatenate([zero, sin] * reps, axis=1)
        sb = jnp.concatenate([-sin, zero] * reps, axis=1)
    return jnp.concatenate([c, sa, sb], axis=1)


def _swa_head_perm():
    return [kvh * SWA_GROUP + g for g in range(SWA_GROUP) for kvh in range(SWA_KV_HEADS)]


def _permute_w_in(w):
    offs = np.cumsum([0, A_W, A_W, A_W, MLA_Q_LORA, MLA_KV_LORA, MLA_ROPE, C_W, 2 * HEAD_DIM, 2 * HEAD_DIM])
    a_q, a_k, a_v, b_cq, b_ckv, b_kr, c_q, c_k, c_v = [w[:, offs[i]:offs[i + 1]] for i in range(9)]
    c_q = jnp.concatenate([c_q[:, h * HEAD_DIM:(h + 1) * HEAD_DIM] for h in _swa_head_perm()], axis=1)
    zeros = lambda n: jnp.zeros((w.shape[0], n), w.dtype)
    kr = jnp.concatenate([zeros(MLA_NOPE), b_kr, zeros(LANES - MLA_NOPE - MLA_ROPE)], axis=1)
    return jnp.concatenate([a_q, a_k, a_v, b_cq, b_ckv, c_q, c_k, c_v, kr], axis=1).astype(BF16)


def _permute_mla(w_uq, w_ukv):
    zq = jnp.zeros((MLA_Q_LORA, LANES - MLA_NOPE - MLA_ROPE), w_uq.dtype)
    zk = jnp.zeros((MLA_KV_LORA, LANES - MLA_NOPE), w_ukv.dtype)
    dq = MLA_NOPE + MLA_ROPE
    dkv = MLA_NOPE + MLA_V
    wq = jnp.concatenate([jnp.concatenate([w_uq[:, h * dq:(h + 1) * dq], zq], axis=1)
                          for h in range(MLA_HEADS)], axis=1)
    wk = jnp.concatenate([jnp.concatenate([w_ukv[:, h * dkv:h * dkv + MLA_NOPE], zk], axis=1)
                          for h in range(MLA_HEADS)], axis=1)
    wv = jnp.concatenate([w_ukv[:, h * dkv + MLA_NOPE:(h + 1) * dkv] for h in range(MLA_HEADS)], axis=1)
    return wq.astype(BF16), jnp.concatenate([wk, wv], axis=1).astype(BF16)


def kernel(x, attn_norm, w_in, na_rpb, mla_q_norm, mla_w_uq, mla_kv_norm, mla_w_ukv, swa_sink, group_norm,
           w_out, ffn_norm, w_router, w_gate, w_up, w_down, final_norm):
    bsz, seq, d = x.shape
    assert (seq, d) == (SEQ, D_MODEL)
    n = bsz * seq
    depth = w_in.shape[0]
    rope_s = _rope_table(HEAD_DIM, 0)
    rope_m = _rope_table(MLA_ROPE, MLA_NOPE)
    c_perm = np.concatenate([np.arange(h * HEAD_DIM, (h + 1) * HEAD_DIM) for h in _swa_head_perm()])

    h2 = x.reshape(n, d)
    for l in range(depth):
        tiled = l > 0
        wq, wkv = _permute_mla(mla_w_uq[l], mla_w_ukv[l])
        aq, ak, av, mq, mk, mv, sq, sk, sv = _proj_call(
            h2, attn_norm[l][None], _permute_w_in(w_in[l]), mla_q_norm[l][None], wq, mla_kv_norm[l][None], wkv,
            rope_s, rope_m, tiled)
        r3 = lambda a: a.reshape(bsz, seq, a.shape[-1])
        o_at = _na_call(r3(aq), r3(ak), av, _na_bias_tables(na_rpb[l]))
        o_bt = _mla_call(r3(mq), r3(mk), mv)
        o_ct = _swa_call(swa_sink[l], r3(sq), r3(sk), sv)

        gn = group_norm[l]
        wo = w_out[l]
        gc = gn[A_W + B_W:][c_perm]
        wc = wo[A_W + B_W:][c_perm]
        hm, xn3, aff_t = _out_call(
            o_at, o_bt, o_ct, h2,
            gn[:A_W, None], gn[A_W:A_W + B_W, None], gc[:, None],
            wo[:A_W].astype(BF16), wo[A_W:A_W + B_W].astype(BF16), wc.astype(BF16),
            ffn_norm[l][None], w_router[l].T, tiled)

        idx, gate = _topk_call(aff_t.reshape(bsz * N_EXPERTS, seq))
        idx_flat = idx.reshape(-1)
        y = _ffn_call(idx_flat, xn3, w_gate, w_up, w_down, l)
        h2 = _combine_call(idx_flat, gate.reshape(-1), y,
                           hm.reshape(bsz, seq * TOKEN_SUB, LANES)).reshape(n * TOKEN_SUB, LANES)
    return _norm_call(h2, final_norm[None]).reshape(bsz, seq, d)
```
